```python
import jax, jax.numpy as jnp
from jax import lax
import numpy as np

D_MODEL = 1024
BATCH = 4
SEQ = 4096
DEPTH = 4
DEC_BATCH = 32
DEC_SEQ = 16
PAST_LEN = 1024

CHUNK = 64
N_A = DEPTH // 2
N_B = DEPTH - N_A
N_DENSE = (DEPTH + 1) // 2
N_MOE = DEPTH // 2
CONV_W = 31
N_HEADS = 16
QK_NOPE = 64
QK_ROPE = 32
V_DIM = 64
KV_RANK = 256
Q_RANK = 384
D_FF = 2816
N_EXPERTS = 8
TOP_K = 2
D_FF_EXPERT = 2816
PLE_DIM = 256
Q_BLOCK = 128
ROPE_THETA = 10000.0
EPS = 1e-6
NEG = -1e30

kernel_name = "yoco_conformer_mla_stream_step"


def rmsnorm(x, g):
    xf = x.astype(jnp.float32)
    y = xf * lax.rsqrt(jnp.mean(xf * xf, axis=-1, keepdims=True) + EPS)
    return (y * g.astype(jnp.float32)).astype(x.dtype)


def layernorm(x, g, b):
    xf = x.astype(jnp.float32)
    mu = jnp.mean(xf, axis=-1, keepdims=True)
    xc = xf - mu
    var = jnp.mean(xc * xc, axis=-1, keepdims=True)
    y = xc * lax.rsqrt(var + EPS) * g.astype(jnp.float32) + b.astype(jnp.float32)
    return y.astype(x.dtype)


def rope_tables(pos):
    inv = ROPE_THETA ** (-jnp.arange(0, QK_ROPE, 2, dtype=jnp.float32) / QK_ROPE)
    ang = pos.astype(jnp.float32)[:, None] * inv[None, :]
    ang = jnp.concatenate([ang, ang], axis=-1)
    return jnp.cos(ang), jnp.sin(ang)


def apply_rope(x, cos, sin):
    xf = x.astype(jnp.float32)
    x1, x2 = jnp.split(xf, 2, axis=-1)
    rot = jnp.concatenate([-x2, x1], axis=-1)
    return (xf * cos + rot * sin).astype(x.dtype)


def conv_module(xn, left, w_pw1, b_pw1, w_dw, b_dw, ln_g, ln_b, w_pw2, b_pw2):
    a = xn @ w_pw1 + b_pw1
    u = a[..., :D_MODEL] * jax.nn.sigmoid(a[..., D_MODEL:])
    full = jnp.concatenate([left.astype(u.dtype), u], axis=1)
    y = lax.conv_general_dilated(full, w_dw.astype(full.dtype)[:, None, :], window_strides=(1,),
                                 padding='VALID', dimension_numbers=('NWC', 'WIO', 'NWC'),
                                 feature_group_count=D_MODEL) + b_dw
    y = jax.nn.silu(layernorm(y, ln_g, ln_b))
    return y @ w_pw2 + b_pw2, full[:, -(CONV_W - 1):]


def swiglu(x, wg, wu, wd):
    return (jax.nn.silu(x @ wg) * (x @ wu)) @ wd


def moe(x, w_router, we_gate, we_up, we_down):
    logits = (x @ w_router).astype(jnp.float32)
    top_v, top_i = lax.top_k(logits, TOP_K)
    top_w = jax.nn.softmax(top_v, axis=-1)
    gates = jnp.sum(jax.nn.one_hot(top_i, N_EXPERTS, dtype=jnp.float32) * top_w[..., None], axis=-2)
    gates = gates.astype(x.dtype)
    y = jnp.zeros_like(x)
    for e in range(N_EXPERTS):
        y = y + gates[..., e:e + 1] * swiglu(x, we_gate[e], we_up[e], we_down[e])
    return y


def shared_latent(h, pos, g_kv, w_dkv, g_ckv):
    z = rmsnorm(h, g_kv) @ w_dkv
    c = rmsnorm(z[..., :KV_RANK], g_ckv)
    cos, sin = rope_tables(pos)
    k_pe = apply_rope(z[..., KV_RANK:], cos[None], sin[None])
    return c, k_pe


def mla_queries(xn, pos, w_dq, g_q, w_uq):
    cq = rmsnorm(xn @ w_dq, g_q)
    q = jnp.einsum('btr,rhd->bthd', cq, w_uq)
    cos, sin = rope_tables(pos)
    q_pe = apply_rope(q[..., QK_NOPE:], cos[None, :, None], sin[None, :, None])
    return q[..., :QK_NOPE], q_pe


def mla_prompt(q_nope, q_pe, k_nope, k_pe, v):
    B, S = q_nope.shape[0], q_nope.shape[1]
    nb = S // Q_BLOCK
    scale = (QK_NOPE + QK_ROPE) ** -0.5
    key_chunk = jnp.arange(S) // CHUNK

    def blk(args):
        qn, qp, qi = args
        s = jnp.einsum('bqhd,bkhd->bhqk', qn, k_nope) + jnp.einsum('bqhr,bkr->bhqk', qp, k_pe)
        s = s.astype(jnp.float32) * scale
        q_chunk = (qi * Q_BLOCK + jnp.arange(Q_BLOCK)) // CHUNK
        s = jnp.where(key_chunk[None, :] <= q_chunk[:, None], s, NEG)
        p = jax.nn.softmax(s, axis=-1).astype(v.dtype)
        return jnp.einsum('bhqk,bkhd->bqhd', p, v)

    qn_b = q_nope.reshape(B, nb, Q_BLOCK, N_HEADS, QK_NOPE).transpose(1, 0, 2, 3, 4)
    qp_b = q_pe.reshape(B, nb, Q_BLOCK, N_HEADS, QK_ROPE).transpose(1, 0, 2, 3, 4)
    out = lax.map(blk, (qn_b, qp_b, jnp.arange(nb)))
    return out.transpose(1, 0, 2, 3, 4).reshape(B, S, N_HEADS, V_DIM)


def mla_sample(q_nope, q_pe, c_all, kpe_all, w_uk, w_uv):
    scale = (QK_NOPE + QK_ROPE) ** -0.5
    q_lat = jnp.einsum('bqhd,chd->bqhc', q_nope, w_uk)
    s = jnp.einsum('bqhc,bkc->bhqk', q_lat, c_all) + jnp.einsum('bqhr,bkr->bhqk', q_pe, kpe_all)
    p = jax.nn.softmax(s.astype(jnp.float32) * scale, axis=-1).astype(c_all.dtype)
    o_lat = jnp.einsum('bhqk,bkc->bqhc', p, c_all)
    return jnp.einsum('bqhc,chd->bqhd', o_lat, w_uv)


def trunk(h, p, pos, conv_left, past, W):
    conv_states = []
    c = k_pe = k_nope = v = c_all = kpe_all = None
    for i in range(DEPTH):
        xn = rmsnorm(h, W['g_mix'][i])
        if i < N_A:
            a = i
            out, st = conv_module(xn, conv_left[a], W['w_pw1'][a], W['b_pw1'][a], W['w_dw'][a],
                                  W['b_dw'][a], W['ln_g'][a], W['ln_b'][a], W['w_pw2'][a], W['b_pw2'][a])
            conv_states.append(st)
        else:
            b = i - N_A
            q_nope, q_pe = mla_queries(xn, pos, W['w_dq'][b], W['g_q'][b], W['w_uq'][b])
            if past is None:
                o = mla_prompt(q_nope, q_pe, k_nope, k_pe, v)
            else:
                o = mla_sample(q_nope, q_pe, c_all, kpe_all, W['w_uk'], W['w_uv'])
            out = jnp.einsum('bthd,hdm->btm', o, W['w_o'][b])
        h = h + out
        xn = rmsnorm(h, W['g_ffn'][i])
        if i % 2 == 0:
            j = i // 2
            h = h + swiglu(xn, W['w_gate'][j], W['w_up'][j], W['w_down'][j])
        else:
            j = i // 2
            h = h + moe(xn, W['w_router'][j], W['we_gate'][j], W['we_up'][j], W['we_down'][j])
        gate = jax.nn.sigmoid(rmsnorm(h, W['g_ple'][i]) @ W['w_ple_gate'][i])
        h = h + (p[i] @ W['w_ple'][i]) * gate
        if i == N_A - 1:
            c, k_pe = shared_latent(h, pos, W['g_kv'], W['w_dkv'], W['g_ckv'])
            if past is None:
                k_nope = jnp.einsum('btc,chd->bthd', c, W['w_uk'])
                v = jnp.einsum('btc,chd->bthd', c, W['w_uv'])
            else:
                c_all = jnp.concatenate([past[0].astype(c.dtype), c], axis=1)
                kpe_all = jnp.concatenate([past[1].astype(k_pe.dtype), k_pe], axis=1)
    return rmsnorm(h, W['g_final']), jnp.stack(conv_states), c, k_pe


def setup_inputs(seed: int = 0) -> dict:
    key = jax.random.key(seed)
    ks = iter(jax.random.split(key, 48))
    D = D_MODEL

    def nrm(shape, fan_in):
        return jax.random.normal(next(ks), shape, jnp.float32) * (fan_in ** -0.5)

    def gain(shape):
        return 1.0 + 0.01 * jax.random.normal(next(ks), shape, jnp.float32)

    def bias(shape):
        return 0.01 * jax.random.normal(next(ks), shape, jnp.float32)

    return {
        "x_prompt": nrm((BATCH, SEQ, D), 1),
        "x_sample": nrm((DEC_BATCH, DEC_SEQ, D), 1),
        "state_conv": nrm((N_A, DEC_BATCH, CONV_W - 1, D), 1),
        "cache_ckv": nrm((DEC_BATCH, PAST_LEN, KV_RANK), 1),
        "cache_kpe": nrm((DEC_BATCH, PAST_LEN, QK_ROPE), 1),
        "p_prompt": nrm((DEPTH, BATCH, SEQ, PLE_DIM), 1),
        "p_sample": nrm((DEPTH, DEC_BATCH, DEC_SEQ, PLE_DIM), 1),
        "g_mix": gain((DEPTH, D)),
        "g_ffn": gain((DEPTH, D)),
        "w_pw1": nrm((N_A, D, 2 * D), D),
        "b_pw1": bias((N_A, 2 * D)),
        "w_dw": nrm((N_A, CONV_W, D), CONV_W),
        "b_dw": bias((N_A, D)),
        "ln_g": gain((N_A, D)),
        "ln_b": bias((N_A, D)),
        "w_pw2": nrm((N_A, D, D), D),
        "b_pw2": bias((N_A, D)),
        "g_kv": gain((D,)),
        "w_dkv": nrm((D, KV_RANK + QK_ROPE), D),
        "g_ckv": gain((KV_RANK,)),
        "w_uk": nrm((KV_RANK, N_HEADS, QK_NOPE), KV_RANK),
        "w_uv": nrm((KV_RANK, N_HEADS, V_DIM), KV_RANK),
        "w_dq": nrm((N_B, D, Q_RANK), D),
        "g_q": gain((N_B, Q_RANK)),
        "w_uq": nrm((N_B, Q_RANK, N_HEADS, QK_NOPE + QK_ROPE), Q_RANK),
        "w_o": nrm((N_B, N_HEADS, V_DIM, D), N_HEADS * V_DIM),
        "w_gate": nrm((N_DENSE, D, D_FF), D),
        "w_up": nrm((N_DENSE, D, D_FF), D),
        "w_down": nrm((N_DENSE, D_FF, D), D_FF),
        "w_router": nrm((N_MOE, D, N_EXPERTS), D),
        "we_gate": nrm((N_MOE, N_EXPERTS, D, D_FF_EXPERT), D),
        "we_up": nrm((N_MOE, N_EXPERTS, D, D_FF_EXPERT), D),
        "we_down": nrm((N_MOE, N_EXPERTS, D_FF_EXPERT, D), D_FF_EXPERT),
        "w_ple": nrm((DEPTH, PLE_DIM, D), PLE_DIM),
        "w_ple_gate": nrm((DEPTH, D, D), D),
        "g_ple": gain((DEPTH, D)),
        "g_final": gain((D,)),
    }


def reference(x_prompt, x_sample, state_conv, cache_ckv, cache_kpe, p_prompt, p_sample,
              g_mix, g_ffn, w_pw1, b_pw1, w_dw, b_dw, ln_g, ln_b, w_pw2, b_pw2,
              g_kv, w_dkv, g_ckv, w_uk, w_uv, w_dq, g_q, w_uq, w_o,
              w_gate, w_up, w_down, w_router, we_gate, we_up, we_down,
              w_ple, w_ple_gate, g_ple, g_final):
    W = dict(g_mix=g_mix, g_ffn=g_ffn, w_pw1=w_pw1, b_pw1=b_pw1, w_dw=w_dw, b_dw=b_dw,
             ln_g=ln_g, ln_b=ln_b, w_pw2=w_pw2, b_pw2=b_pw2, g_kv=g_kv, w_dkv=w_dkv,
             g_ckv=g_ckv, w_uk=w_uk, w_uv=w_uv, w_dq=w_dq, g_q=g_q, w_uq=w_uq, w_o=w_o,
             w_gate=w_gate, w_up=w_up, w_down=w_down, w_router=w_router, we_gate=we_gate,
             we_up=we_up, we_down=we_down, w_ple=w_ple, w_ple_gate=w_ple_gate,
             g_ple=g_ple, g_final=g_final)
    bp, sp = x_prompt.shape[0], x_prompt.shape[1]
    pos_p = jnp.arange(sp)
    left_p = jnp.zeros((N_A, bp, CONV_W - 1, D_MODEL), x_prompt.dtype)
    y_prompt, conv_state_prompt, ckv_prompt, kpe_prompt = trunk(x_prompt, p_prompt, pos_p, left_p, None, W)
    ss = x_sample.shape[1]
    pos_s = PAST_LEN + jnp.arange(ss)
    y_sample, conv_state_sample, ckv_sample, kpe_sample = trunk(
        x_sample, p_sample, pos_s, state_conv, (cache_ckv, cache_kpe), W)
    return (y_prompt, y_sample, conv_state_prompt, conv_state_sample, ckv_prompt, kpe_prompt, ckv_sample, kpe_sample)
```

```python
import functools
import math

import jax
import jax.numpy as jnp
from jax import lax
from jax.experimental import pallas as pl
from jax.experimental.pallas import tpu as pltpu

EPS = 1e-6
NEG = -1e30
CHUNK = 64
ROPE_THETA = 10000.0
TOP_K = 2

LANES = 128
HALO = 32
MOE_TILE = 256
VMEM_LIMIT = 60 * 1024 * 1024

F32 = jnp.float32
BF16 = jnp.bfloat16


def _cparams(sem=None):
    return pltpu.CompilerParams(dimension_semantics=sem, vmem_limit_bytes=VMEM_LIMIT)


def _const_spec(shape):
    nd = len(shape)
    return pl.BlockSpec(shape, lambda *_: (0,) * nd, pipeline_mode=pl.Buffered(1))


def _rms(x, g):
    return x * lax.rsqrt(jnp.mean(x * x, axis=-1, keepdims=True) + EPS) * g


def _dot(a, b):
    return jnp.dot(a, b, preferred_element_type=F32)


def _dot_nt(a, b):
    return lax.dot_general(a, b, (((1,), (1,)), ((), ())), preferred_element_type=F32)


def _sigmoid(x):
    return 1.0 / (1.0 + jnp.exp(-x))


def _silu(x):
    return x * _sigmoid(x)


def _ff_chunk(f):
    best = None
    for c in range(LANES, min(f, 1536) + 1, LANES):
        if f % c == 0:
            best = c
    return best if best is not None else f


def _pw1_kernel(h_ref, g_ref, w_ref, b_ref, u_ref):
    d = h_ref.shape[1]
    xn = _rms(h_ref[...], g_ref[...]).astype(BF16)
    a = _dot(xn, w_ref[...]) + b_ref[...]
    u_ref[...] = a[:, :d] * _sigmoid(a[:, d:])


def _pw1_glu(h, g, w, b, tm):
    t, d = h.shape
    return pl.pallas_call(
        _pw1_kernel,
        out_shape=jax.ShapeDtypeStruct((t, d), F32),
        grid=(t // tm,),
        in_specs=[pl.BlockSpec((tm, d), lambda i: (i, 0)),
                  _const_spec((1, d)), _const_spec((d, 2 * d)), _const_spec((1, 2 * d))],
        out_specs=pl.BlockSpec((tm, d), lambda i: (i, 0)),
        compiler_params=_cparams(("parallel",)),
        name="pw1_glu",
    )(h, g, w, b)


def _conv_kernel(cur_ref, prev_ref, h_ref, wdw_ref, bdw_ref, lg_ref, lb_ref, w2_ref, b2_ref,
                 o_ref, full_ref, y_ref, *, taps, zero_first):
    tt, d = cur_ref.shape
    prev = prev_ref[...]
    if zero_first:
        prev = jnp.where(pl.program_id(1) == 0, 0.0, prev)
    full_ref[0:HALO, :] = prev
    full_ref[HALO:HALO + tt, :] = cur_ref[...]
    off = HALO - (taps - 1)
    rc = min(tt, 32)
    for c in range(d // LANES):
        lanes = slice(c * LANES, (c + 1) * LANES)
        wk = [wdw_ref[k:k + 1, lanes] for k in range(taps)]
        for r in range(tt // rc):
            acc = wk[0] * full_ref[r * rc + off:r * rc + off + rc, lanes]
            for k in range(1, taps):
                acc = acc + wk[k] * full_ref[r * rc + off + k:r * rc + off + k + rc, lanes]
            y_ref[r * rc:(r + 1) * rc, lanes] = acc
    y = y_ref[...] + bdw_ref[...]
    mu = jnp.mean(y, axis=-1, keepdims=True)
    yc = y - mu
    var = jnp.mean(yc * yc, axis=-1, keepdims=True)
    z = _silu(yc * lax.rsqrt(var + EPS) * lg_ref[...] + lb_ref[...])
    o_ref[...] = h_ref[...] + _dot(z.astype(BF16), w2_ref[...]) + b2_ref[...]


def _conv_block(u, h, st_pad, wdw, bdw, lg, lb, w2, b2, *, n_b, seq, n_bs, seq_s, tt):
    t, d = h.shape
    tp = n_b * seq
    taps = wdw.shape[0]
    nblk = seq // tt
    consts = [_const_spec(wdw.shape), _const_spec((1, d)), _const_spec((1, d)), _const_spec((1, d)),
              _const_spec((d, d)), _const_spec((1, d))]
    scratch = lambda rows: [pltpu.VMEM((HALO + rows, d), F32), pltpu.VMEM((rows, d), F32)]
    cur_p = lambda b, i: (b * nblk + i, 0)
    h = pl.pallas_call(
        functools.partial(_conv_kernel, taps=taps, zero_first=True),
        out_shape=jax.ShapeDtypeStruct((t, d), F32),
        grid=(n_b, nblk),
        in_specs=[pl.BlockSpec((tt, d), cur_p),
                  pl.BlockSpec((HALO, d), lambda b, i: (jnp.maximum((b * seq + i * tt) // HALO - 1, 0), 0)),
                  pl.BlockSpec((tt, d), cur_p)] + consts,
        out_specs=pl.BlockSpec((tt, d), cur_p),
        scratch_shapes=scratch(tt),
        input_output_aliases={2: 0},
        compiler_params=_cparams(("parallel", "parallel")),
        name="conv_prompt",
    )(u, u, h, wdw, bdw, lg, lb, w2, b2)
    cur_s = lambda b: (tp // seq_s + b, 0)
    return pl.pallas_call(
        functools.partial(_conv_kernel, taps=taps, zero_first=False),
        out_shape=jax.ShapeDtypeStruct((t, d), F32),
        grid=(n_bs,),
        in_specs=[pl.BlockSpec((seq_s, d), cur_s),
                  pl.BlockSpec((HALO, d), lambda b: (b, 0)),
                  pl.BlockSpec((seq_s, d), cur_s)] + consts,
        out_specs=pl.BlockSpec((seq_s, d), cur_s),
        scratch_shapes=scratch(seq_s),
        input_output_aliases={2: 0},
        compiler_params=_cparams(("parallel",)),
        name="conv_sample",
    )(u, st_pad, h, wdw, bdw, lg, lb, w2, b2)


def _ple(h, p, gp, wpg, wple):
    gate = _sigmoid(_dot(_rms(h, gp).astype(BF16), wpg))
    return h + _dot(p.astype(BF16), wple) * gate


def _p_specs(layer, tm, n_p, ple):
    return [pl.BlockSpec((None, tm, ple), lambda i, *_: (layer, jnp.minimum(i, n_p - 1), 0)),
            pl.BlockSpec((None, tm, ple), lambda i, *_: (layer, jnp.maximum(i - n_p, 0), 0))]


def _ffn_kernel(h_ref, g_ref, wg_ref, wu_ref, wd_ref, pp_ref, ps_ref, gp_ref, wpg_ref, wple_ref,
                o_ref, act_ref, *, n_p, fc):
    h = h_ref[...]
    xn = _rms(h, g_ref[...]).astype(BF16)
    f = wg_ref.shape[1]
    for c in range(f // fc):
        cols = slice(c * fc, (c + 1) * fc)
        act_ref[:, cols] = (_silu(_dot(xn, wg_ref[:, cols])) * _dot(xn, wu_ref[:, cols])).astype(BF16)
    h = h + _dot(act_ref[...], wd_ref[...])
    p = jnp.where(pl.program_id(0) < n_p, pp_ref[...], ps_ref[...])
    o_ref[...] = _ple(h, p, gp_ref[...], wpg_ref[...], wple_ref[...])


def _ffn_ple(h, g, wg, wu, wd, pp, ps, layer, gp, wpg, wple, tm, n_p):
    t, d = h.shape
    f = wg.shape[1]
    ple = pp.shape[-1]
    return pl.pallas_call(
        functools.partial(_ffn_kernel, n_p=n_p, fc=_ff_chunk(f)),
        out_shape=jax.ShapeDtypeStruct((t, d), F32),
        grid=(t // tm,),
        in_specs=[pl.BlockSpec((tm, d), lambda i: (i, 0)), _const_spec((1, d)),
                  _const_spec((d, f)), _const_spec((d, f)), _const_spec((f, d))]
                 + _p_specs(layer, tm, n_p, ple)
                 + [_const_spec((1, d)), _const_spec((d, d)), _const_spec((ple, d))],
        out_specs=pl.BlockSpec((tm, d), lambda i: (i, 0)),
        scratch_shapes=[pltpu.VMEM((tm, f), BF16)],
        compiler_params=_cparams(("parallel",)),
        name="ffn_ple",
    )(h, g, wg, wu, wd, pp, ps, gp, wpg, wple)


def _router_kernel(h_ref, g_ref, whi_ref, wlo_ref, meta_ref, cnt_ref, run_ref, *, n_exp):
    i = pl.program_id(0)

    @pl.when(i == 0)
    def _():
        run_ref[...] = jnp.zeros_like(run_ref)

    tm = h_ref.shape[0]
    xn = _rms(h_ref[...], g_ref[...])
    hi = xn.astype(BF16)
    lo = (xn - hi.astype(F32)).astype(BF16)
    logits = _dot(hi, whi_ref[...]) + (_dot(hi, wlo_ref[...]) + _dot(lo, whi_ref[...]))
    lane = lax.broadcasted_iota(jnp.int32, (tm, LANES), 1).astype(F32)
    logits = jnp.where(lane < n_exp, logits, -jnp.inf)
    m1 = jnp.max(logits, axis=-1, keepdims=True)
    i1 = jnp.min(jnp.where(logits == m1, lane, float(LANES)), axis=-1, keepdims=True)
    rest = jnp.where(lane == i1, -jnp.inf, logits)
    m2 = jnp.max(rest, axis=-1, keepdims=True)
    i2 = jnp.min(jnp.where(rest == m2, lane, float(LANES)), axis=-1, keepdims=True)
    e2 = jnp.exp(m2 - m1)
    w1 = 1.0 / (1.0 + e2)
    w2 = e2 / (1.0 + e2)
    oh1 = (lane == i1).astype(F32)
    oh2 = (lane == i2).astype(F32)
    oh = oh1 + oh2
    rows = lax.broadcasted_iota(jnp.int32, (tm, tm), 0)
    cols = lax.broadcasted_iota(jnp.int32, (tm, tm), 1)
    tri = jnp.where(cols < rows, 1.0, 0.0).astype(BF16)
    pre = _dot(tri, oh.astype(BF16)) + run_ref[...]
    r1 = jnp.sum(pre * oh1, axis=-1, keepdims=True)
    r2 = jnp.sum(pre * oh2, axis=-1, keepdims=True)
    run = run_ref[...] + jnp.sum(oh, axis=0, keepdims=True)
    run_ref[...] = run
    cnt_ref[...] = jnp.broadcast_to(run, cnt_ref.shape)
    meta = jnp.where(lane == 0, i1, 0.0)
    meta = jnp.where(lane == 1, i2, meta)
    meta = jnp.where(lane == 2, r1, meta)
    meta = jnp.where(lane == 3, r2, meta)
    meta = jnp.where(lane == 4, w1, meta)
    meta = jnp.where(lane == 5, w2, meta)
    meta_ref[...] = meta


def _router(h, g, whi, wlo, n_exp, tm):
    t, d = h.shape
    return pl.pallas_call(
        functools.partial(_router_kernel, n_exp=n_exp),
        out_shape=(jax.ShapeDtypeStruct((t, LANES), F32), jax.ShapeDtypeStruct((8, LANES), F32)),
        grid=(t // tm,),
        in_specs=[pl.BlockSpec((tm, d), lambda i: (i, 0)), _const_spec((1, d)),
                  _const_spec((d, LANES)), _const_spec((d, LANES))],
        out_specs=(pl.BlockSpec((tm, LANES), lambda i: (i, 0)), pl.BlockSpec((8, LANES), lambda i: (0, 0))),
        scratch_shapes=[pltpu.VMEM((1, LANES), F32)],
        compiler_params=_cparams(("arbitrary",)),
        name="router",
    )(h, g, whi, wlo)


def _dispatch_kernel(pos_ref, lt_ref, h_ref, g_ref, xs_ref, xn_ref, z_ref, sem, zsem, *, n_exp):
    i = pl.program_id(0)
    tm = h_ref.shape[0]

    @pl.when(i == 0)
    def _():
        z_ref[...] = jnp.zeros_like(z_ref)
        for e in range(2 * n_exp):
            @pl.when(lt_ref[e] >= 0)
            def _():
                start = pl.multiple_of(lt_ref[e] * MOE_TILE, MOE_TILE)
                cp = pltpu.make_async_copy(z_ref, xs_ref.at[pl.ds(start, MOE_TILE)], zsem)
                cp.start()
                cp.wait()

    xn_ref[...] = _rms(h_ref[...], g_ref[...])

    def issue(t, carry):
        for k in range(TOP_K):
            p = pos_ref[(i * tm + t) * TOP_K + k]
            pltpu.make_async_copy(xn_ref.at[pl.ds(t, 1)], xs_ref.at[pl.ds(p, 1)], sem).start()
        return carry

    lax.fori_loop(0, tm, issue, 0)
    for k in range(TOP_K):
        pltpu.make_async_copy(xn_ref, xs_ref.at[pl.ds(0, tm)], sem).wait()


def _dispatch(pos, last_tile, h, g, rows, tm):
    t, d = h.shape
    return pl.pallas_call(
        functools.partial(_dispatch_kernel, n_exp=last_tile.shape[0] // 2),
        out_shape=jax.ShapeDtypeStruct((rows, d), F32),
        grid_spec=pltpu.PrefetchScalarGridSpec(
            num_scalar_prefetch=2,
            grid=(t // tm,),
            in_specs=[pl.BlockSpec((tm, d), lambda i, pos, lt: (i, 0)),
                      pl.BlockSpec((1, d), lambda i, pos, lt: (0, 0))],
            out_specs=pl.BlockSpec(memory_space=pl.ANY),
            scratch_shapes=[pltpu.VMEM((tm, d), F32), pltpu.VMEM((MOE_TILE, d), F32),
                            pltpu.SemaphoreType.DMA(()), pltpu.SemaphoreType.DMA(())]),
        compiler_params=_cparams(("arbitrary",)),
        name="moe_dispatch",
    )(pos, last_tile, h, g)


def _moe_kernel(te_ref, tb_ref, nt_ref, x_ref, wg_ref, wu_ref, wd_ref, y_ref, act_ref, *, fc):
    @pl.when(pl.program_id(0) >= nt_ref[0])
    def _():
        y_ref[...] = jnp.zeros_like(y_ref)

    @pl.when(pl.program_id(0) < nt_ref[0])
    def _():
        x = x_ref[...].astype(BF16)
        f = wg_ref.shape[1]
        for c in range(f // fc):
            cols = slice(c * fc, (c + 1) * fc)
            act_ref[:, cols] = (_silu(_dot(x, wg_ref[:, cols])) * _dot(x, wu_ref[:, cols])).astype(BF16)
        y_ref[...] = _dot(act_ref[...], wd_ref[...])


def _moe_ffn(tile_e, tile_b, n_tiles, xs, weg, weu, wed, layer):
    rows, d = xs.shape
    f = weg.shape[-1]
    nt_max = tile_e.shape[0]
    wmap = lambda i, te, tb, nt: (layer, te[i], 0, 0)
    xmap = lambda i, te, tb, nt: (tb[i], 0)
    return pl.pallas_call(
        functools.partial(_moe_kernel, fc=_ff_chunk(f)),
        out_shape=jax.ShapeDtypeStruct((rows, d), F32),
        grid_spec=pltpu.PrefetchScalarGridSpec(
            num_scalar_prefetch=3,
            grid=(nt_max,),
            in_specs=[pl.BlockSpec((MOE_TILE, d), xmap),
                      pl.BlockSpec((None, None, d, f), wmap),
                      pl.BlockSpec((None, None, d, f), wmap),
                      pl.BlockSpec((None, None, f, d), wmap)],
            out_specs=pl.BlockSpec((MOE_TILE, d), lambda i, te, tb, nt: (i, 0)),
            scratch_shapes=[pltpu.VMEM((MOE_TILE, f), BF16)]),
        compiler_params=_cparams(("arbitrary",)),
        name="moe_ffn",
    )(tile_e, tile_b, n_tiles, xs, weg, weu, wed)


def _combine_kernel(pos_ref, h_ref, meta_ref, ys_ref, pp_ref, ps_ref, gp_ref, wpg_ref, wple_ref, gf_ref,
                    o_ref, yk_ref, sem, *, n_p, final):
    i = pl.program_id(0)
    tm = h_ref.shape[0]

    def issue(t, carry):
        for k in range(TOP_K):
            p = pos_ref[(i * tm + t) * TOP_K + k]
            pltpu.make_async_copy(ys_ref.at[pl.ds(p, 1)], yk_ref.at[k, pl.ds(t, 1)], sem).start()
        return carry

    lax.fori_loop(0, tm, issue, 0)
    for k in range(TOP_K):
        pltpu.make_async_copy(ys_ref.at[pl.ds(0, tm)], yk_ref.at[k], sem).wait()
    meta = meta_ref[...]
    h = h_ref[...] + (meta[:, 4:5] * yk_ref[0] + meta[:, 5:6] * yk_ref[1])
    p = jnp.where(i < n_p, pp_ref[...], ps_ref[...])
    h = _ple(h, p, gp_ref[...], wpg_ref[...], wple_ref[...])
    if final:
        h = _rms(h, gf_ref[...])
    o_ref[...] = h


def _combine_ple(pos, h, meta, ys, pp, ps, layer, gp, wpg, wple, gf, tm, n_p, final):
    t, d = h.shape
    ple = pp.shape[-1]
    c2 = lambda shape: pl.BlockSpec(shape, lambda i, pos: (0, 0))
    return pl.pallas_call(
        functools.partial(_combine_kernel, n_p=n_p, final=final),
        out_shape=jax.ShapeDtypeStruct((t, d), F32),
        grid_spec=pltpu.PrefetchScalarGridSpec(
            num_scalar_prefetch=1,
            grid=(t // tm,),
            in_specs=[pl.BlockSpec((tm, d), lambda i, pos: (i, 0)),
                      pl.BlockSpec((tm, LANES), lambda i, pos: (i, 0)),
                      pl.BlockSpec(memory_space=pl.ANY)]
                     + _p_specs(layer, tm, n_p, ple)
                     + [c2((1, d)), c2((d, d)), c2((ple, d)), c2((1, d))],
            out_specs=pl.BlockSpec((tm, d), lambda i, pos: (i, 0)),
            scratch_shapes=[pltpu.VMEM((TOP_K, tm, d), F32), pltpu.SemaphoreType.DMA(())]),
        compiler_params=_cparams(("arbitrary",)),
        name="moe_combine",
    )(pos, h, meta, ys, pp, ps, gp, wpg, wple, gf)


def _latent_kernel(h_ref, g_ref, w_ref, gc_ref, cos_ref, sin_ref, wk_ref, ek_ref, wv_ref,
                   c_ref, kpe_ref, k_ref, v_ref):
    r = c_ref.shape[1]
    xn = _rms(h_ref[...], g_ref[...]).astype(BF16)
    z = _dot(xn, w_ref[...])
    c = _rms(z[:, :r], gc_ref[...])
    kpe = z[:, r:r + LANES] * cos_ref[...] + z[:, r + LANES:r + 2 * LANES] * sin_ref[...]
    c_ref[...] = c
    kpe_ref[...] = kpe
    cb = c.astype(BF16)
    k_ref[...] = (_dot(cb, wk_ref[...]) + _dot(kpe.astype(BF16), ek_ref[...])).astype(BF16)
    v_ref[...] = _dot(cb, wv_ref[...]).astype(BF16)


def _tbl_spec(tm, n_p, n_sblk):
    return pl.BlockSpec((tm, LANES), lambda i: (jnp.where(i < n_p, i % n_sblk, n_sblk + i - n_p), 0))


def _latent(h, g, w, gc, cos_t, sin_t, wk, ek, wv, tm, n_p, n_sblk):
    t, d = h.shape
    r = gc.shape[1]
    hk = wk.shape[1]
    row = lambda w_: pl.BlockSpec((tm, w_), lambda i: (i, 0))
    return pl.pallas_call(
        _latent_kernel,
        out_shape=(jax.ShapeDtypeStruct((t, r), F32), jax.ShapeDtypeStruct((t, LANES), F32),
                   jax.ShapeDtypeStruct((t, hk), BF16), jax.ShapeDtypeStruct((t, hk), BF16)),
        grid=(t // tm,),
        in_specs=[row(d), _const_spec((1, d)), _const_spec(w.shape), _const_spec((1, r)),
                  _tbl_spec(tm, n_p, n_sblk), _tbl_spec(tm, n_p, n_sblk),
                  _const_spec(wk.shape), _const_spec(ek.shape), _const_spec(wv.shape)],
        out_specs=(row(r), row(LANES), row(hk), row(hk)),
        compiler_params=_cparams(("parallel",)),
        name="mla_latent",
    )(h, g, w, gc, cos_t, sin_t, wk, ek, wv)


def _query_kernel(h_ref, g_ref, wdq_ref, gq_ref, wq_ref, wqr_ref, cos_ref, sin_ref, q_ref, *, scale):
    xn = _rms(h_ref[...], g_ref[...]).astype(BF16)
    cq = _rms(_dot(xn, wdq_ref[...]), gq_ref[...]).astype(BF16)
    q = _dot(cq, wq_ref[...])
    qr = _dot(cq, wqr_ref[...])
    cos = cos_ref[...] * scale
    sin = sin_ref[...] * scale
    for hd in range(q.shape[1] // LANES):
        lanes = slice(hd * LANES, (hd + 1) * LANES)
        q_ref[:, lanes] = (q[:, lanes] * cos + qr[:, lanes] * sin).astype(BF16)


def _queries(h, g, wdq, gq, wq, wqr, cos_t, sin_t, scale, tm, n_p, n_sblk):
    t, d = h.shape
    hk = wq.shape[1]
    return pl.pallas_call(
        functools.partial(_query_kernel, scale=scale),
        out_shape=jax.ShapeDtypeStruct((t, hk), BF16),
        grid=(t // tm,),
        in_specs=[pl.BlockSpec((tm, d), lambda i: (i, 0)), _const_spec((1, d)), _const_spec(wdq.shape),
                  _const_spec(gq.shape), _const_spec(wq.shape), _const_spec(wqr.shape),
                  _tbl_spec(tm, n_p, n_sblk), _tbl_spec(tm, n_p, n_sblk)],
        out_specs=pl.BlockSpec((tm, hk), lambda i: (i, 0)),
        compiler_params=_cparams(("parallel",)),
        name="mla_queries",
    )(h, g, wdq, gq, wq, wqr, cos_t, sin_t)


def _attn_kernel(q_ref, k_ref, v_ref, o_ref, *, tq, tk):
    qi = pl.program_id(2)
    out = None
    for hh in range(2):
        lanes = slice(hh * LANES, (hh + 1) * LANES)
        q = q_ref[:, lanes]

        def step(j, carry, masked):
            m, l, acc = carry
            start = pl.multiple_of(j * tk, tk)
            s = _dot_nt(q, k_ref[pl.ds(start, tk), lanes])
            if masked:
                qpos = qi * tq + lax.broadcasted_iota(jnp.int32, (tq, tk), 0)
                kpos = j * tk + lax.broadcasted_iota(jnp.int32, (tq, tk), 1)
                shift = CHUNK.bit_length() - 1
                s = jnp.where(lax.shift_right_logical(kpos, shift) <= lax.shift_right_logical(qpos, shift), s, NEG)
            m_new = jnp.maximum(m, jnp.max(s, axis=-1, keepdims=True))
            alpha = jnp.exp(m - m_new)
            p = jnp.exp(s - m_new)
            l = alpha * l + jnp.sum(p, axis=-1, keepdims=True)
            acc = alpha * acc + _dot(p.astype(BF16), v_ref[pl.ds(start, tk), lanes])
            return m_new, l, acc

        carry = (jnp.full((tq, 1), NEG, F32), jnp.zeros((tq, 1), F32), jnp.zeros((tq, LANES), F32))
        n_full = qi * (tq // tk)
        carry = lax.fori_loop(0, n_full, functools.partial(step, masked=False), carry)
        for dj in range(tq // tk):
            carry = step(n_full + dj, carry, True)
        _, l, acc = carry
        o = acc / l
        out = o if out is None else out + o
    o_ref[...] = out.astype(BF16)


def _prompt_attention(q, k, v, n_b, seq, n_heads, tq, tk):
    nq = seq // tq
    hv = n_heads * (LANES // 2)
    return pl.pallas_call(
        functools.partial(_attn_kernel, tq=tq, tk=tk),
        out_shape=jax.ShapeDtypeStruct((n_b * seq, hv), BF16),
        grid=(n_b, n_heads // 2, nq),
        in_specs=[pl.BlockSpec((tq, 2 * LANES), lambda b, hp, i: (b * nq + i, hp)),
                  pl.BlockSpec((seq, 2 * LANES), lambda b, hp, i: (b, hp)),
                  pl.BlockSpec((seq, 2 * LANES), lambda b, hp, i: (b, hp))],
        out_specs=pl.BlockSpec((tq, LANES), lambda b, hp, i: (b * nq + i, hp)),
        compiler_params=_cparams(("parallel", "parallel", "arbitrary")),
        name="prompt_attention",
    )(q, k, v)


def _qfeat_kernel(q_ref, w_ref, o_ref):
    o_ref[...] = _dot(q_ref[...], w_ref[...]).astype(BF16)


def _sample_qfeat(q, wq2f, row0_blk, ts):
    n_heads, _, fw = wq2f.shape
    return pl.pallas_call(
        _qfeat_kernel,
        out_shape=jax.ShapeDtypeStruct((n_heads, ts, fw), BF16),
        grid=(n_heads,),
        in_specs=[pl.BlockSpec((ts, LANES), lambda hd: (row0_blk, hd)),
                  pl.BlockSpec((None, LANES, fw), lambda hd: (hd, 0, 0))],
        out_specs=pl.BlockSpec((None, ts, fw), lambda hd: (hd, 0, 0)),
        compiler_params=_cparams(("parallel",)),
        name="sample_qfeat",
    )(q, wq2f)


def _sample_attn_kernel(qf_ref, cc_ref, ckpe_ref, cn_ref, kn_ref, e_ref, o_ref):
    n_heads, sq, fw = qf_ref.shape
    r = cc_ref.shape[1]
    qf = qf_ref[...].reshape(n_heads * sq, fw)
    qc, qp = qf[:, :r], qf[:, r:]
    kc = cc_ref[...].astype(BF16)
    kp = _dot(ckpe_ref[...].astype(BF16), e_ref[...]).astype(BF16)
    kcn = cn_ref[...].astype(BF16)
    kpn = kn_ref[...].astype(BF16)
    s1 = _dot_nt(qc, kc) + _dot_nt(qp, kp)
    s2 = _dot_nt(qc, kcn) + _dot_nt(qp, kpn)
    m = jnp.maximum(jnp.max(s1, axis=-1, keepdims=True), jnp.max(s2, axis=-1, keepdims=True))
    p1 = jnp.exp(s1 - m)
    p2 = jnp.exp(s2 - m)
    l = jnp.sum(p1, axis=-1, keepdims=True) + jnp.sum(p2, axis=-1, keepdims=True)
    o = (_dot(p1.astype(BF16), kc) + _dot(p2.astype(BF16), kcn)) / l
    o_ref[...] = o.astype(BF16).reshape(n_heads, sq, r)


def _sample_attention(qf, cache_c, cache_kpe, c_all, kpe_all, e_place, tp, sq):
    n_heads, ts, fw = qf.shape
    n_bs, past, r = cache_c.shape
    rope = cache_kpe.shape[-1]
    return pl.pallas_call(
        _sample_attn_kernel,
        out_shape=jax.ShapeDtypeStruct((n_heads, ts, r), BF16),
        grid=(n_bs,),
        in_specs=[pl.BlockSpec((n_heads, sq, fw), lambda b: (0, b, 0)),
                  pl.BlockSpec((None, past, r), lambda b: (b, 0, 0)),
                  pl.BlockSpec((None, past, rope), lambda b: (b, 0, 0)),
                  pl.BlockSpec((sq, r), lambda b: (tp // sq + b, 0)),
                  pl.BlockSpec((sq, LANES), lambda b: (tp // sq + b, 0)),
                  _const_spec(e_place.shape)],
        out_specs=pl.BlockSpec((n_heads, sq, r), lambda b: (0, b, 0)),
        compiler_params=_cparams(("parallel",)),
        name="sample_attention",
    )(qf, cache_c, cache_kpe, c_all, kpe_all, e_place)


def _sample_uv_kernel(ol_ref, wv_ref, o_ref):
    o_ref[...] = (_dot(ol_ref[0], wv_ref[0]) + _dot(ol_ref[1], wv_ref[1])).astype(BF16)


def _sample_uv(o_lat, wv_pair):
    n_heads, ts, r = o_lat.shape
    return pl.pallas_call(
        _sample_uv_kernel,
        out_shape=jax.ShapeDtypeStruct((ts, n_heads * (LANES // 2)), BF16),
        grid=(n_heads // 2,),
        in_specs=[pl.BlockSpec((2, ts, r), lambda hp: (hp, 0, 0)),
                  pl.BlockSpec((2, r, LANES), lambda hp: (hp, 0, 0))],
        out_specs=pl.BlockSpec((ts, LANES), lambda hp: (0, hp)),
        compiler_params=_cparams(("parallel",)),
        name="sample_uv",
    )(o_lat, wv_pair)


def _oproj_kernel(op_ref, os_ref, w_ref, h_ref, out_ref, *, n_p):
    o = jnp.where(pl.program_id(0) < n_p, op_ref[...], os_ref[...])
    out_ref[...] = h_ref[...] + _dot(o, w_ref[...])


def _oproj(o_p, o_s, w, h, tm, n_p):
    t, d = h.shape
    hv = o_p.shape[1]
    return pl.pallas_call(
        functools.partial(_oproj_kernel, n_p=n_p),
        out_shape=jax.ShapeDtypeStruct((t, d), F32),
        grid=(t // tm,),
        in_specs=[pl.BlockSpec((tm, hv), lambda i: (jnp.minimum(i, n_p - 1), 0)),
                  pl.BlockSpec((tm, hv), lambda i: (jnp.maximum(i - n_p, 0), 0)),
                  _const_spec(w.shape),
                  pl.BlockSpec((tm, d), lambda i: (i, 0))],
        out_specs=pl.BlockSpec((tm, d), lambda i: (i, 0)),
        compiler_params=_cparams(("parallel",)),
        name="attn_oproj",
    )(o_p, o_s, w, h)


def _rot_cols(w):
    half = w.shape[-1] // 2
    return jnp.concatenate([-w[..., half:], w[..., :half]], axis=-1)


def _moe_schedule(meta, cnt, n_exp, nt_max):
    counts = cnt[0, :n_exp].astype(jnp.int32)
    ntile = (counts + MOE_TILE - 1) // MOE_TILE
    tile_end = jnp.cumsum(ntile)
    seg_start = (tile_end - ntile) * MOE_TILE
    n_tiles = tile_end[-1]
    e = meta[:, 0:TOP_K].astype(jnp.int32)
    rank = meta[:, TOP_K:2 * TOP_K].astype(jnp.int32)
    pos = (seg_start[e] + rank).reshape(-1)
    tid = jnp.minimum(jnp.arange(nt_max, dtype=jnp.int32), n_tiles - 1)
    tile_e = jnp.sum((tid[:, None] >= tile_end[None, :]).astype(jnp.int32), axis=1)
    seg_last = jnp.where(ntile > 0, tile_end - 1, -1)
    tail = n_tiles + jnp.arange(n_exp, dtype=jnp.int32)
    tail = jnp.where(tail < nt_max, tail, -1)
    zero_tiles = jnp.concatenate([seg_last, tail]).astype(jnp.int32)
    return pos, tile_e, tid, n_tiles.reshape(1), zero_tiles


def kernel(x_prompt, x_sample, state_conv, cache_ckv, cache_kpe, p_prompt, p_sample, g_mix, g_ffn, w_pw1, b_pw1, w_dw, b_dw, ln_g, ln_b, w_pw2, b_pw2, g_kv, w_dkv, g_ckv, w_uk, w_uv, w_dq, g_q, w_uq, w_o, w_gate, w_up, w_down, w_router, we_gate, we_up, we_down, w_ple, w_ple_gate, g_ple, g_final):
    n_b, seq, d = x_prompt.shape
    n_bs, seq_s, _ = x_sample.shape
    depth = g_mix.shape[0]
    n_a = w_pw1.shape[0]
    taps = w_dw.shape[1]
    past = cache_ckv.shape[1]
    r_kv = cache_ckv.shape[2]
    rope = cache_kpe.shape[2]
    n_heads, nope = w_uk.shape[1], w_uk.shape[2]
    v_dim = w_uv.shape[2]
    n_exp = w_router.shape[2]
    ple = p_prompt.shape[-1]
    tp, ts = n_b * seq, n_bs * seq_s
    t = tp + ts
    tm = ts
    assert tp % tm == 0 and seq % tm == 0 and tm % MOE_TILE == 0
    assert nope == LANES // 2 and v_dim == LANES // 2 and nope + rope <= LANES and taps - 1 <= HALO
    n_p = tp // tm
    n_sblk = seq // tm
    scale = float(nope + rope) ** -0.5
    row = lambda a: a.reshape(1, -1)
    bf = lambda a: a.astype(BF16)

    h = jnp.concatenate([x_prompt.reshape(tp, d), x_sample.reshape(ts, d)], axis=0)
    pp = p_prompt.reshape(depth, tp, ple)
    ps = p_sample.reshape(depth, ts, ple)

    posn = jnp.concatenate([jnp.arange(seq), past + (jnp.arange(ts) % seq_s)]).astype(F32)
    inv = ROPE_THETA ** (-jnp.arange(0, rope, 2, dtype=F32) / rope)
    ang = posn[:, None] * inv[None, :]
    ang = jnp.concatenate([ang, ang], axis=-1)
    pad_r = LANES - nope - rope
    cos_t = jnp.concatenate([jnp.ones((posn.shape[0], nope), F32), jnp.cos(ang),
                             jnp.zeros((posn.shape[0], pad_r), F32)], axis=1)
    sin_t = jnp.concatenate([jnp.zeros((posn.shape[0], nope), F32), jnp.sin(ang),
                             jnp.zeros((posn.shape[0], pad_r), F32)], axis=1)

    conv_p, conv_s = [], []
    c_all = kpe_all = k_cat = v_cat = None
    nt_max = (TOP_K * t) // MOE_TILE + n_exp
    rows_pad = nt_max * MOE_TILE

    for i in range(depth):
        if i < n_a:
            a = i
            u = _pw1_glu(h, row(g_mix[i]), bf(w_pw1[a]), row(b_pw1[a]), tm)
            st_pad = jnp.pad(state_conv[a], ((0, 0), (HALO - (taps - 1), 0), (0, 0))).reshape(n_bs * HALO, d)
            h = _conv_block(u, h, st_pad, w_dw[a], row(b_dw[a]), row(ln_g[a]), row(ln_b[a]),
                            bf(w_pw2[a]), row(b_pw2[a]), n_b=n_b, seq=seq, n_bs=n_bs, seq_s=seq_s, tt=tm)
            conv_p.append(u[:tp].reshape(n_b, seq, d)[:, seq - (taps - 1):])
            us = u[tp:].reshape(n_bs, seq_s, d)
            conv_s.append(jnp.concatenate([state_conv[a], us], axis=1)[:, -(taps - 1):])
        else:
            b = i - n_a
            wq = w_uq[b]
            zq = jnp.zeros(wq.shape[:2] + (pad_r,), F32)
            wq_cat = jnp.concatenate([wq, zq], axis=-1).reshape(wq.shape[0], n_heads * LANES)
            wq_rot = jnp.concatenate([jnp.zeros(wq.shape[:2] + (nope,), F32), _rot_cols(wq[..., nope:]), zq],
                                     axis=-1).reshape(wq.shape[0], n_heads * LANES)
            q = _queries(h, row(g_mix[i]), bf(w_dq[b]), row(g_q[b]), bf(wq_cat), bf(wq_rot),
                         cos_t, sin_t, scale, tm, n_p, n_sblk)
            o_p = _prompt_attention(q, k_cat, v_cat, n_b, seq, n_heads, tq=tm, tk=MOE_TILE)
            wk_t = jnp.transpose(w_uk, (1, 2, 0))
            sel = jnp.zeros((LANES, LANES), F32).at[jnp.arange(nope, nope + rope), jnp.arange(nope, nope + rope)].set(1.0)
            wq2f = jnp.concatenate([
                jnp.concatenate([wk_t, jnp.zeros((n_heads, LANES - nope, r_kv), F32)], axis=1),
                jnp.broadcast_to(sel, (n_heads, LANES, LANES))], axis=2)
            qf = _sample_qfeat(q, bf(wq2f), n_p, ts)
            e_place = jnp.zeros((rope, LANES), F32).at[jnp.arange(rope), nope + jnp.arange(rope)].set(1.0)
            o_lat = _sample_attention(qf, cache_ckv, cache_kpe, c_all, kpe_all, bf(e_place), tp, seq_s)
            wv_h = jnp.transpose(w_uv, (1, 0, 2))
            zv = jnp.zeros_like(wv_h)
            wv_pair = jnp.where((jnp.arange(n_heads) % 2 == 0)[:, None, None],
                                jnp.concatenate([wv_h, zv], axis=2), jnp.concatenate([zv, wv_h], axis=2))
            o_s = _sample_uv(o_lat, bf(wv_pair))
            h = _oproj(o_p, o_s, bf(w_o[b].reshape(n_heads * v_dim, d)), h, tm, n_p)

        j = i // 2
        last = i == depth - 1
        if i % 2 == 0:
            h = _ffn_ple(h, row(g_ffn[i]), bf(w_gate[j]), bf(w_up[j]), bf(w_down[j]), pp, ps, i,
                         row(g_ple[i]), bf(w_ple_gate[i]), bf(w_ple[i]), tm, n_p)
        else:
            wr = jnp.pad(w_router[j], ((0, 0), (0, LANES - n_exp)))
            wr_hi = bf(wr)
            wr_lo = bf(wr - wr_hi.astype(F32))
            meta, cnt = _router(h, row(g_ffn[i]), wr_hi, wr_lo, n_exp, tm)
            pos, tile_e, tile_b, n_tiles, zero_tiles = _moe_schedule(meta, cnt, n_exp, nt_max)
            xs = _dispatch(pos, zero_tiles, h, row(g_ffn[i]), rows_pad, tm)
            ys = _moe_ffn(tile_e, tile_b, n_tiles, xs, bf(we_gate), bf(we_up), bf(we_down), j)
            h = _combine_ple(pos, h, meta, ys, pp, ps, i, row(g_ple[i]), bf(w_ple_gate[i]), bf(w_ple[i]),
                             row(g_final), tm, n_p, final=last)
        if last and i % 2 == 0:
            raise NotImplementedError("final norm is fused into the MoE combine of the last layer")

        if i == n_a - 1:
            w_ext = jnp.concatenate([
                w_dkv[:, :r_kv],
                jnp.zeros((d, nope), F32), w_dkv[:, r_kv:], jnp.zeros((d, pad_r), F32),
                jnp.zeros((d, nope), F32), _rot_cols(w_dkv[:, r_kv:]), jnp.zeros((d, pad_r), F32)], axis=1)
            wk_cat = jnp.concatenate([w_uk, jnp.zeros((r_kv, n_heads, LANES - nope), F32)], axis=2)
            wk_cat = wk_cat.reshape(r_kv, n_heads * LANES)
            lane = jnp.arange(LANES)
            e_k = ((lane[:, None] == (jnp.arange(n_heads * LANES) % LANES)[None, :])
                   & (lane[:, None] >= nope) & (lane[:, None] < nope + rope)).astype(F32)
            zv = jnp.zeros_like(w_uv)
            wv_cat = jnp.where((jnp.arange(n_heads) % 2 == 0)[None, :, None],
                               jnp.concatenate([w_uv, zv], axis=2), jnp.concatenate([zv, w_uv], axis=2))
            wv_cat = wv_cat.reshape(r_kv, n_heads * LANES)
            c_all, kpe_all, k_cat, v_cat = _latent(h, row(g_kv), bf(w_ext), row(g_ckv), cos_t, sin_t,
                                                   bf(wk_cat), bf(e_k), bf(wv_cat), tm, n_p, n_sblk)

    kpe_out = kpe_all[:, nope:nope + rope]
    return (h[:tp].reshape(n_b, seq, d), h[tp:].reshape(n_bs, seq_s, d),
            jnp.stack(conv_p), jnp.stack(conv_s),
            c_all[:tp].reshape(n_b, seq, r_kv), kpe_out[:tp].reshape(n_b, seq, rope),
            c_all[tp:].reshape(n_bs, seq_s, r_kv), kpe_out[tp:].reshape(n_bs, seq_s, rope))
```

```python
import functools
import math

import jax
import jax.numpy as jnp
from jax import lax
from jax.experimental import pallas as pl
from jax.experimental.pallas import tpu as pltpu

EPS = 1e-6
NEG = -1e30
CHUNK = 64
ROPE_THETA = 10000.0
TOP_K = 2

LANES = 128
HALO = 32
MOE_TILE = 256
VMEM_LIMIT = 60 * 1024 * 1024

F32 = jnp.float32
BF16 = jnp.bfloat16


def _cparams(sem=None):
    return pltpu.CompilerParams(dimension_semantics=sem, vmem_limit_bytes=VMEM_LIMIT)


def _const_spec(shape):
    nd = len(shape)
    return pl.BlockSpec(shape, lambda *_: (0,) * nd, pipeline_mode=pl.Buffered(1))


def _rms(x, g):
    return x * lax.rsqrt(jnp.mean(x * x, axis=-1, keepdims=True) + EPS) * g


def _dot(a, b):
    return jnp.dot(a, b, preferred_element_type=F32)


def _dot_nt(a, b):
    return lax.dot_general(a, b, (((1,), (1,)), ((), ())), preferred_element_type=F32)


def _sigmoid(x):
    return 1.0 / (1.0 + jnp.exp(-x))


def _silu(x):
    return x * _sigmoid(x)


def _ff_chunk(f):
    best = None
    for c in range(LANES, min(f, 1536) + 1, LANES):
        if f % c == 0:
            best = c
    return best if best is not None else f


def _pw1_kernel(h_ref, g_ref, w_ref, b_ref, u_ref):
    d = h_ref.shape[1]
    xn = _rms(h_ref[...], g_ref[...]).astype(BF16)
    a = _dot(xn, w_ref[...]) + b_ref[...]
    u_ref[...] = a[:, :d] * _sigmoid(a[:, d:])


def _pw1_glu(h, g, w, b, tm):
    t, d = h.shape
    return pl.pallas_call(
        _pw1_kernel,
        out_shape=jax.ShapeDtypeStruct((t, d), F32),
        grid=(t // tm,),
        in_specs=[pl.BlockSpec((tm, d), lambda i: (i, 0)),
                  _const_spec((1, d)), _const_spec((d, 2 * d)), _const_spec((1, 2 * d))],
        out_specs=pl.BlockSpec((tm, d), lambda i: (i, 0)),
        compiler_params=_cparams(("parallel",)),
        name="pw1_glu",
    )(h, g, w, b)


def _conv_kernel(cur_ref, prev_ref, h_ref, wdw_ref, bdw_ref, lg_ref, lb_ref, w2_ref, b2_ref,
                 o_ref, full_ref, y_ref, *, taps, zero_first):
    tt, d = cur_ref.shape
    prev = prev_ref[...]
    if zero_first:
        prev = jnp.where(pl.program_id(1) == 0, 0.0, prev)
    full_ref[0:HALO, :] = prev
    full_ref[HALO:HALO + tt, :] = cur_ref[...]
    off = HALO - (taps - 1)
    rc = min(tt, 32)
    for c in range(d // LANES):
        lanes = slice(c * LANES, (c + 1) * LANES)
        wk = [wdw_ref[k:k + 1, lanes] for k in range(taps)]
        for r in range(tt // rc):
            acc = wk[0] * full_ref[r * rc + off:r * rc + off + rc, lanes]
            for k in range(1, taps):
                acc = acc + wk[k] * full_ref[r * rc + off + k:r * rc + off + k + rc, lanes]
            y_ref[r * rc:(r + 1) * rc, lanes] = acc
    y = y_ref[...] + bdw_ref[...]
    mu = jnp.mean(y, axis=-1, keepdims=True)
    yc = y - mu
    var = jnp.mean(yc * yc, axis=-1, keepdims=True)
    z = _silu(yc * lax.rsqrt(var + EPS) * lg_ref[...] + lb_ref[...])
    o_ref[...] = h_ref[...] + _dot(z.astype(BF16), w2_ref[...]) + b2_ref[...]


def _conv_block(u, h, st_pad, wdw, bdw, lg, lb, w2, b2, *, n_b, seq, n_bs, seq_s, tt):
    t, d = h.shape
    tp = n_b * seq
    taps = wdw.shape[0]
    nblk = seq // tt
    consts = [_const_spec(wdw.shape), _const_spec((1, d)), _const_spec((1, d)), _const_spec((1, d)),
              _const_spec((d, d)), _const_spec((1, d))]
    scratch = lambda rows: [pltpu.VMEM((HALO + rows, d), F32), pltpu.VMEM((rows, d), F32)]
    cur_p = lambda b, i: (b * nblk + i, 0)
    h = pl.pallas_call(
        functools.partial(_conv_kernel, taps=taps, zero_first=True),
        out_shape=jax.ShapeDtypeStruct((t, d), F32),
        grid=(n_b, nblk),
        in_specs=[pl.BlockSpec((tt, d), cur_p),
                  pl.BlockSpec((HALO, d), lambda b, i: (jnp.maximum((b * seq + i * tt) // HALO - 1, 0), 0)),
                  pl.BlockSpec((tt, d), cur_p)] + consts,
        out_specs=pl.BlockSpec((tt, d), cur_p),
        scratch_shapes=scratch(tt),
        input_output_aliases={2: 0},
        compiler_params=_cparams(("parallel", "parallel")),
        name="conv_prompt",
    )(u, u, h, wdw, bdw, lg, lb, w2, b2)
    cur_s = lambda b: (tp // seq_s + b, 0)
    return pl.pallas_call(
        functools.partial(_conv_kernel, taps=taps, zero_first=False),
        out_shape=jax.ShapeDtypeStruct((t, d), F32),
        grid=(n_bs,),
        in_specs=[pl.BlockSpec((seq_s, d), cur_s),
                  pl.BlockSpec((HALO, d), lambda b: (b, 0)),
                  pl.BlockSpec((seq_s, d), cur_s)] + consts,
        out_specs=pl.BlockSpec((seq_s, d), cur_s),
        scratch_shapes=scratch(seq_s),
        input_output_aliases={2: 0},
        compiler_params=_cparams(("parallel",)),
        name="conv_sample",
    )(u, st_pad, h, wdw, bdw, lg, lb, w2, b2)


def _ple(h, p, gp, wpg, wple):
    gate = _sigmoid(_dot(_rms(h, gp).astype(BF16), wpg))
    return h + _dot(p.astype(BF16), wple) * gate


def _p_specs(layer, tm, n_p, ple):
    return [pl.BlockSpec((None, tm, ple), lambda i, *_: (layer, jnp.minimum(i, n_p - 1), 0)),
            pl.BlockSpec((None, tm, ple), lambda i, *_: (layer, jnp.maximum(i - n_p, 0), 0))]


def _ffn_kernel(h_ref, g_ref, wg_ref, wu_ref, wd_ref, pp_ref, ps_ref, gp_ref, wpg_ref, wple_ref,
                o_ref, act_ref, *, n_p, fc):
    h = h_ref[...]
    xn = _rms(h, g_ref[...]).astype(BF16)
    f = wg_ref.shape[1]
    for c in range(f // fc):
        cols = slice(c * fc, (c + 1) * fc)
        act_ref[:, cols] = (_silu(_dot(xn, wg_ref[:, cols])) * _dot(xn, wu_ref[:, cols])).astype(BF16)
    h = h + _dot(act_ref[...], wd_ref[...])
    p = jnp.where(pl.program_id(0) < n_p, pp_ref[...], ps_ref[...])
    o_ref[...] = _ple(h, p, gp_ref[...], wpg_ref[...], wple_ref[...])


def _ffn_ple(h, g, wg, wu, wd, pp, ps, layer, gp, wpg, wple, tm, n_p):
    t, d = h.shape
    f = wg.shape[1]
    ple = pp.shape[-1]
    return pl.pallas_call(
        functools.partial(_ffn_kernel, n_p=n_p, fc=_ff_chunk(f)),
        out_shape=jax.ShapeDtypeStruct((t, d), F32),
        grid=(t // tm,),
        in_specs=[pl.BlockSpec((tm, d), lambda i: (i, 0)), _const_spec((1, d)),
                  _const_spec((d, f)), _const_spec((d, f)), _const_spec((f, d))]
                 + _p_specs(layer, tm, n_p, ple)
                 + [_const_spec((1, d)), _const_spec((d, d)), _const_spec((ple, d))],
        out_specs=pl.BlockSpec((tm, d), lambda i: (i, 0)),
        scratch_shapes=[pltpu.VMEM((tm, f), BF16)],
        compiler_params=_cparams(("parallel",)),
        name="ffn_ple",
    )(h, g, wg, wu, wd, pp, ps, gp, wpg, wple)


def _router_kernel(h_ref, g_ref, whi_ref, wlo_ref, meta_ref, cnt_ref, run_ref, *, n_exp):
    i = pl.program_id(0)

    @pl.when(i == 0)
    def _():
        run_ref[...] = jnp.zeros_like(run_ref)

    tm = h_ref.shape[0]
    xn = _rms(h_ref[...], g_ref[...])
    hi = xn.astype(BF16)
    lo = (xn - hi.astype(F32)).astype(BF16)
    logits = _dot(hi, whi_ref[...]) + (_dot(hi, wlo_ref[...]) + _dot(lo, whi_ref[...]))
    lane = lax.broadcasted_iota(jnp.int32, (tm, LANES), 1).astype(F32)
    logits = jnp.where(lane < n_exp, logits, -jnp.inf)
    m1 = jnp.max(logits, axis=-1, keepdims=True)
    i1 = jnp.min(jnp.where(logits == m1, lane, float(LANES)), axis=-1, keepdims=True)
    rest = jnp.where(lane == i1, -jnp.inf, logits)
    m2 = jnp.max(rest, axis=-1, keepdims=True)
    i2 = jnp.min(jnp.where(rest == m2, lane, float(LANES)), axis=-1, keepdims=True)
    e2 = jnp.exp(m2 - m1)
    w1 = 1.0 / (1.0 + e2)
    w2 = e2 / (1.0 + e2)
    oh1 = (lane == i1).astype(F32)
    oh2 = (lane == i2).astype(F32)
    oh = oh1 + oh2
    rows = lax.broadcasted_iota(jnp.int32, (tm, tm), 0)
    cols = lax.broadcasted_iota(jnp.int32, (tm, tm), 1)
    tri = jnp.where(cols < rows, 1.0, 0.0).astype(BF16)
    pre = _dot(tri, oh.astype(BF16)) + run_ref[...]
    r1 = jnp.sum(pre * oh1, axis=-1, keepdims=True)
    r2 = jnp.sum(pre * oh2, axis=-1, keepdims=True)
    run = run_ref[...] + jnp.sum(oh, axis=0, keepdims=True)
    run_ref[...] = run
    cnt_ref[...] = jnp.broadcast_to(run, cnt_ref.shape)
    meta = jnp.where(lane == 0, i1, 0.0)
    meta = jnp.where(lane == 1, i2, meta)
    meta = jnp.where(lane == 2, r1, meta)
    meta = jnp.where(lane == 3, r2, meta)
    meta = jnp.where(lane == 4, w1, meta)
    meta = jnp.where(lane == 5, w2, meta)
    meta_ref[...] = meta


def _router(h, g, whi, wlo, n_exp, tm):
    t, d = h.shape
    return pl.pallas_call(
        functools.partial(_router_kernel, n_exp=n_exp),
        out_shape=(jax.ShapeDtypeStruct((t, LANES), F32), jax.ShapeDtypeStruct((8, LANES), F32)),
        grid=(t // tm,),
        in_specs=[pl.BlockSpec((tm, d), lambda i: (i, 0)), _const_spec((1, d)),
                  _const_spec((d, LANES)), _const_spec((d, LANES))],
        out_specs=(pl.BlockSpec((tm, LANES), lambda i: (i, 0)), pl.BlockSpec((8, LANES), lambda i: (0, 0))),
        scratch_shapes=[pltpu.VMEM((1, LANES), F32)],
        compiler_params=_cparams(("arbitrary",)),
        name="router",
    )(h, g, whi, wlo)


def _dispatch_kernel(pos_ref, lt_ref, h_ref, g_ref, xs_ref, xn_ref, z_ref, sem, zsem, *, n_exp):
    i = pl.program_id(0)
    tm = h_ref.shape[0]

    @pl.when(i == 0)
    def _():
        z_ref[...] = jnp.zeros_like(z_ref)
        for e in range(2 * n_exp):
            @pl.when(lt_ref[e] >= 0)
            def _():
                start = pl.multiple_of(lt_ref[e] * MOE_TILE, MOE_TILE)
                cp = pltpu.make_async_copy(z_ref, xs_ref.at[pl.ds(start, MOE_TILE)], zsem)
                cp.start()
                cp.wait()

    xn_ref[...] = _rms(h_ref[...], g_ref[...])

    def issue(t, carry):
        for k in range(TOP_K):
            p = pos_ref[(i * tm + t) * TOP_K + k]
            pltpu.make_async_copy(xn_ref.at[pl.ds(t, 1)], xs_ref.at[pl.ds(p, 1)], sem).start()
        return carry

    lax.fori_loop(0, tm, issue, 0)
    for k in range(TOP_K):
        pltpu.make_async_copy(xn_ref, xs_ref.at[pl.ds(0, tm)], sem).wait()


def _dispatch(pos, last_tile, h, g, rows, tm):
    t, d = h.shape
    return pl.pallas_call(
        functools.partial(_dispatch_kernel, n_exp=last_tile.shape[0] // 2),
        out_shape=jax.ShapeDtypeStruct((rows, d), F32),
        grid_spec=pltpu.PrefetchScalarGridSpec(
            num_scalar_prefetch=2,
            grid=(t // tm,),
            in_specs=[pl.BlockSpec((tm, d), lambda i, pos, lt: (i, 0)),
                      pl.BlockSpec((1, d), lambda i, pos, lt: (0, 0))],
            out_specs=pl.BlockSpec(memory_space=pl.ANY),
            scratch_shapes=[pltpu.VMEM((tm, d), F32), pltpu.VMEM((MOE_TILE, d), F32),
                            pltpu.SemaphoreType.DMA(()), pltpu.SemaphoreType.DMA(())]),
        compiler_params=_cparams(("arbitrary",)),
        name="moe_dispatch",
    )(pos, last_tile, h, g)


def _moe_kernel(te_ref, tb_ref, nt_ref, x_ref, wg_ref, wu_ref, wd_ref, y_ref, act_ref, *, fc):
    @pl.when(pl.program_id(0) >= nt_ref[0])
    def _():
        y_ref[...] = jnp.zeros_like(y_ref)

    @pl.when(pl.program_id(0) < nt_ref[0])
    def _():
        x = x_ref[...].astype(BF16)
        f = wg_ref.shape[1]
        for c in range(f // fc):
            cols = slice(c * fc, (c + 1) * fc)
            act_ref[:, cols] = (_silu(_dot(x, wg_ref[:, cols])) * _dot(x, wu_ref[:, cols])).astype(BF16)
        y_ref[...] = _dot(act_ref[...], wd_ref[...])


def _moe_ffn(tile_e, tile_b, n_tiles, xs, weg, weu, wed, layer):
    rows, d = xs.shape
    f = weg.shape[-1]
    nt_max = tile_e.shape[0]
    wmap = lambda i, te, tb, nt: (layer, te[i], 0, 0)
    xmap = lambda i, te, tb, nt: (tb[i], 0)
    return pl.pallas_call(
        functools.partial(_moe_kernel, fc=_ff_chunk(f)),
        out_shape=jax.ShapeDtypeStruct((rows, d), F32),
        grid_spec=pltpu.PrefetchScalarGridSpec(
            num_scalar_prefetch=3,
            grid=(nt_max,),
            in_specs=[pl.BlockSpec((MOE_TILE, d), xmap),
                      pl.BlockSpec((None, None, d, f), wmap),
                      pl.BlockSpec((None, None, d, f), wmap),
                      pl.BlockSpec((None, None, f, d), wmap)],
            out_specs=pl.BlockSpec((MOE_TILE, d), lambda i, te, tb, nt: (i, 0)),
            scratch_shapes=[pltpu.VMEM((MOE_TILE, f), BF16)]),
        compiler_params=_cparams(("arbitrary",)),
        name="moe_ffn",
    )(tile_e, tile_b, n_tiles, xs, weg, weu, wed)


def _combine_kernel(pos_ref, h_ref, meta_ref, ys_ref, pp_ref, ps_ref, gp_ref, wpg_ref, wple_ref, gf_ref,
                    o_ref, yk_ref, sem, *, n_p, final):
    i = pl.program_id(0)
    tm = h_ref.shape[0]

    def issue(t, carry):
        for k in range(TOP_K):
            p = pos_ref[(i * tm + t) * TOP_K + k]
            pltpu.make_async_copy(ys_ref.at[pl.ds(p, 1)], yk_ref.at[k, pl.ds(t, 1)], sem).start()
        return carry

    lax.fori_loop(0, tm, issue, 0)
    for k in range(TOP_K):
        pltpu.make_async_copy(ys_ref.at[pl.ds(0, tm)], yk_ref.at[k], sem).wait()
    meta = meta_ref[...]
    h = h_ref[...] + (meta[:, 4:5] * yk_ref[0] + meta[:, 5:6] * yk_ref[1])
    p = jnp.where(i < n_p, pp_ref[...], ps_ref[...])
    h = _ple(h, p, gp_ref[...], wpg_ref[...], wple_ref[...])
    if final:
        h = _rms(h, gf_ref[...])
    o_ref[...] = h


def _combine_ple(pos, h, meta, ys, pp, ps, layer, gp, wpg, wple, gf, tm, n_p, final):
    t, d = h.shape
    ple = pp.shape[-1]
    c2 = lambda shape: pl.BlockSpec(shape, lambda i, pos: (0, 0))
    return pl.pallas_call(
        functools.partial(_combine_kernel, n_p=n_p, final=final),
        out_shape=jax.ShapeDtypeStruct((t, d), F32),
        grid_spec=pltpu.PrefetchScalarGridSpec(
            num_scalar_prefetch=1,
            grid=(t // tm,),
            in_specs=[pl.BlockSpec((tm, d), lambda i, pos: (i, 0)),
                      pl.BlockSpec((tm, LANES), lambda i, pos: (i, 0)),
                      pl.BlockSpec(memory_space=pl.ANY)]
                     + _p_specs(layer, tm, n_p, ple)
                     + [c2((1, d)), c2((d, d)), c2((ple, d)), c2((1, d))],
            out_specs=pl.BlockSpec((tm, d), lambda i, pos: (i, 0)),
            scratch_shapes=[pltpu.VMEM((TOP_K, tm, d), F32), pltpu.SemaphoreType.DMA(())]),
        compiler_params=_cparams(("arbitrary",)),
        name="moe_combine",
    )(pos, h, meta, ys, pp, ps, gp, wpg, wple, gf)


def _latent_kernel(h_ref, g_ref, w_ref, gc_ref, cos_ref, sin_ref, wk_ref, ek_ref, wv_ref, one_ref,
                   c_ref, kpe_ref, k_ref, v_ref):
    r = c_ref.shape[1]
    xn = _rms(h_ref[...], g_ref[...]).astype(BF16)
    z = _dot(xn, w_ref[...])
    c = _rms(z[:, :r], gc_ref[...])
    kpe = z[:, r:r + LANES] * cos_ref[...] + z[:, r + LANES:r + 2 * LANES] * sin_ref[...]
    c_ref[...] = c
    kpe_ref[...] = kpe
    cb = c.astype(BF16)
    k_ref[...] = (_dot(cb, wk_ref[...]) + _dot(kpe.astype(BF16), ek_ref[...])).astype(BF16)
    v_ref[...] = (_dot(cb, wv_ref[...]) + one_ref[...]).astype(BF16)


def _tbl_spec(tm, n_p, n_sblk):
    return pl.BlockSpec((tm, LANES), lambda i: (jnp.where(i < n_p, i % n_sblk, n_sblk + i - n_p), 0))


def _latent(h, g, w, gc, cos_t, sin_t, wk, ek, wv, ones, tm, n_p, n_sblk):
    t, d = h.shape
    r = gc.shape[1]
    hk = wk.shape[1]
    row = lambda w_: pl.BlockSpec((tm, w_), lambda i: (i, 0))
    return pl.pallas_call(
        _latent_kernel,
        out_shape=(jax.ShapeDtypeStruct((t, r), F32), jax.ShapeDtypeStruct((t, LANES), F32),
                   jax.ShapeDtypeStruct((t, hk), BF16), jax.ShapeDtypeStruct((t, hk), BF16)),
        grid=(t // tm,),
        in_specs=[row(d), _const_spec((1, d)), _const_spec(w.shape), _const_spec((1, r)),
                  _tbl_spec(tm, n_p, n_sblk), _tbl_spec(tm, n_p, n_sblk),
                  _const_spec(wk.shape), _const_spec(ek.shape), _const_spec(wv.shape), _const_spec(ones.shape)],
        out_specs=(row(r), row(LANES), row(hk), row(hk)),
        compiler_params=_cparams(("parallel",)),
        name="mla_latent",
    )(h, g, w, gc, cos_t, sin_t, wk, ek, wv, ones)


def _query_kernel(h_ref, g_ref, wdq_ref, gq_ref, wq_ref, wqr_ref, cos_ref, sin_ref, q_ref, *, scale):
    xn = _rms(h_ref[...], g_ref[...]).astype(BF16)
    cq = _rms(_dot(xn, wdq_ref[...]), gq_ref[...]).astype(BF16)
    q = _dot(cq, wq_ref[...])
    qr = _dot(cq, wqr_ref[...])
    cos = cos_ref[...] * scale
    sin = sin_ref[...] * scale
    for hd in range(q.shape[1] // LANES):
        lanes = slice(hd * LANES, (hd + 1) * LANES)
        q_ref[:, lanes] = (q[:, lanes] * cos + qr[:, lanes] * sin).astype(BF16)


def _queries(h, g, wdq, gq, wq, wqr, cos_t, sin_t, scale, tm, n_p, n_sblk):
    t, d = h.shape
    hk = wq.shape[1]
    return pl.pallas_call(
        functools.partial(_query_kernel, scale=scale),
        out_shape=jax.ShapeDtypeStruct((t, hk), BF16),
        grid=(t // tm,),
        in_specs=[pl.BlockSpec((tm, d), lambda i: (i, 0)), _const_spec((1, d)), _const_spec(wdq.shape),
                  _const_spec(gq.shape), _const_spec(wq.shape), _const_spec(wqr.shape),
                  _tbl_spec(tm, n_p, n_sblk), _tbl_spec(tm, n_p, n_sblk)],
        out_specs=pl.BlockSpec((tm, hk), lambda i: (i, 0)),
        compiler_params=_cparams(("parallel",)),
        name="mla_queries",
    )(h, g, wdq, gq, wq, wqr, cos_t, sin_t)


def _attn_kernel(q_ref, k_ref, v_ref, o_ref, s00, s01, s10, s11, acc0, acc1, m0, m1, *, tq, tk):
    qi = pl.program_id(2)
    half = LANES // 2
    s_refs = ((s00, s01), (s10, s11))
    acc_refs = (acc0, acc1)
    m_refs = (m0, m1)
    lanes = (slice(0, LANES), slice(LANES, 2 * LANES))
    shift = CHUNK.bit_length() - 1

    def produce(hd, slot, j):
        start = pl.multiple_of(j * tk, tk)
        s_refs[hd][slot][...] = _dot_nt(q_ref[:, lanes[hd]], k_ref[pl.ds(start, tk), lanes[hd]])

    def consume(hd, slot, j, masked):
        start = pl.multiple_of(j * tk, tk)
        s = s_refs[hd][slot][...]
        if masked:
            qpos = qi * tq + lax.broadcasted_iota(jnp.int32, (tq, tk), 0)
            kpos = j * tk + lax.broadcasted_iota(jnp.int32, (tq, tk), 1)
            s = jnp.where(lax.shift_right_logical(kpos, shift) <= lax.shift_right_logical(qpos, shift), s, NEG)
        m_old = m_refs[hd][...]
        m_new = jnp.maximum(m_old, jnp.broadcast_to(jnp.max(s, axis=-1, keepdims=True), m_old.shape))
        p = jnp.exp2(s - jnp.concatenate([m_new] * (tk // LANES), axis=1)).astype(BF16)
        acc_refs[hd][...] = (jnp.exp2(m_old - m_new) * acc_refs[hd][...]
                             + _dot(p, v_ref[pl.ds(start, tk), lanes[hd]]))
        m_refs[hd][...] = m_new

    for hd in range(2):
        m_refs[hd][...] = jnp.full(m_refs[hd].shape, NEG, F32)
        acc_refs[hd][...] = jnp.zeros(acc_refs[hd].shape, F32)
        produce(hd, 0, 0)

    assert tq == tk

    def pair(jj, carry):
        a = 2 * jj
        for hd in range(2):
            produce(hd, 1, a + 1)
        for hd in range(2):
            consume(hd, 0, a, False)
        for hd in range(2):
            produce(hd, 0, a + 2)
        for hd in range(2):
            consume(hd, 1, a + 1, False)
        return carry

    lax.fori_loop(0, qi // 2, pair, 0)
    a = 2 * (qi // 2)

    @pl.when(qi % 2 == 0)
    def _():
        for hd in range(2):
            consume(hd, 0, a, True)

    @pl.when(qi % 2 == 1)
    def _():
        for hd in range(2):
            produce(hd, 1, a + 1)
        for hd in range(2):
            consume(hd, 0, a, False)
        for hd in range(2):
            consume(hd, 1, a + 1, True)

    acc_e, acc_o = acc0[...], acc1[...]
    o_e = acc_e / acc_e[:, half:half + 1]
    o_o = acc_o / acc_o[:, 0:1]
    lane = lax.broadcasted_iota(jnp.int32, (tq, LANES), 1)
    o_ref[...] = jnp.where(lane < half, o_e, o_o).astype(BF16)


def _prompt_attention(q, k, v, n_b, seq, n_heads, tq, tk):
    nq = seq // tq
    hv = n_heads * (LANES // 2)
    return pl.pallas_call(
        functools.partial(_attn_kernel, tq=tq, tk=tk),
        out_shape=jax.ShapeDtypeStruct((n_b * seq, hv), BF16),
        grid=(n_b, n_heads // 2, nq),
        in_specs=[pl.BlockSpec((tq, 2 * LANES), lambda b, hp, i: (b * nq + i, hp)),
                  pl.BlockSpec((seq, 2 * LANES), lambda b, hp, i: (b, hp)),
                  pl.BlockSpec((seq, 2 * LANES), lambda b, hp, i: (b, hp))],
        out_specs=pl.BlockSpec((tq, LANES), lambda b, hp, i: (b * nq + i, hp)),
        scratch_shapes=[pltpu.VMEM((tq, tk), F32)] * 4 + [pltpu.VMEM((tq, LANES), F32)] * 2
                       + [pltpu.VMEM((tq, LANES), F32)] * 2,
        compiler_params=_cparams(("parallel", "parallel", "arbitrary")),
        name="prompt_attention",
    )(q, k, v)


def _qfeat_kernel(q_ref, w_ref, o_ref):
    o_ref[...] = _dot(q_ref[...], w_ref[...]).astype(BF16)


def _sample_qfeat(q, wq2f, row0_blk, ts):
    n_heads, _, fw = wq2f.shape
    return pl.pallas_call(
        _qfeat_kernel,
        out_shape=jax.ShapeDtypeStruct((n_heads, ts, fw), BF16),
        grid=(n_heads,),
        in_specs=[pl.BlockSpec((ts, LANES), lambda hd: (row0_blk, hd)),
                  pl.BlockSpec((None, LANES, fw), lambda hd: (hd, 0, 0))],
        out_specs=pl.BlockSpec((None, ts, fw), lambda hd: (hd, 0, 0)),
        compiler_params=_cparams(("parallel",)),
        name="sample_qfeat",
    )(q, wq2f)


def _sample_attn_kernel(qf_ref, cc_ref, ckpe_ref, cn_ref, kn_ref, e_ref, o_ref):
    n_heads, sq, fw = qf_ref.shape
    r = cc_ref.shape[1]
    qf = qf_ref[...].reshape(n_heads * sq, fw)
    qc, qp = qf[:, :r], qf[:, r:]
    kc = cc_ref[...].astype(BF16)
    kp = _dot(ckpe_ref[...].astype(BF16), e_ref[...]).astype(BF16)
    kcn = cn_ref[...].astype(BF16)
    kpn = kn_ref[...].astype(BF16)
    s1 = _dot_nt(qc, kc) + _dot_nt(qp, kp)
    s2 = _dot_nt(qc, kcn) + _dot_nt(qp, kpn)
    m = jnp.maximum(jnp.max(s1, axis=-1, keepdims=True), jnp.max(s2, axis=-1, keepdims=True))
    p1 = jnp.exp2(s1 - m)
    p2 = jnp.exp2(s2 - m)
    l = jnp.sum(p1, axis=-1, keepdims=True) + jnp.sum(p2, axis=-1, keepdims=True)
    o = (_dot(p1.astype(BF16), kc) + _dot(p2.astype(BF16), kcn)) / l
    o_ref[...] = o.astype(BF16).reshape(n_heads, sq, r)


def _sample_attention(qf, cache_c, cache_kpe, c_all, kpe_all, e_place, tp, sq):
    n_heads, ts, fw = qf.shape
    n_bs, past, r = cache_c.shape
    rope = cache_kpe.shape[-1]
    return pl.pallas_call(
        _sample_attn_kernel,
        out_shape=jax.ShapeDtypeStruct((n_heads, ts, r), BF16),
        grid=(n_bs,),
        in_specs=[pl.BlockSpec((n_heads, sq, fw), lambda b: (0, b, 0)),
                  pl.BlockSpec((None, past, r), lambda b: (b, 0, 0)),
                  pl.BlockSpec((None, past, rope), lambda b: (b, 0, 0)),
                  pl.BlockSpec((sq, r), lambda b: (tp // sq + b, 0)),
                  pl.BlockSpec((sq, LANES), lambda b: (tp // sq + b, 0)),
                  _const_spec(e_place.shape)],
        out_specs=pl.BlockSpec((n_heads, sq, r), lambda b: (0, b, 0)),
        compiler_params=_cparams(("parallel",)),
        name="sample_attention",
    )(qf, cache_c, cache_kpe, c_all, kpe_all, e_place)


def _sample_uv_kernel(ol_ref, wv_ref, o_ref):
    o_ref[...] = (_dot(ol_ref[0], wv_ref[0]) + _dot(ol_ref[1], wv_ref[1])).astype(BF16)


def _sample_uv(o_lat, wv_pair):
    n_heads, ts, r = o_lat.shape
    return pl.pallas_call(
        _sample_uv_kernel,
        out_shape=jax.ShapeDtypeStruct((ts, n_heads * (LANES // 2)), BF16),
        grid=(n_heads // 2,),
        in_specs=[pl.BlockSpec((2, ts, r), lambda hp: (hp, 0, 0)),
                  pl.BlockSpec((2, r, LANES), lambda hp: (hp, 0, 0))],
        out_specs=pl.BlockSpec((ts, LANES), lambda hp: (0, hp)),
        compiler_params=_cparams(("parallel",)),
        name="sample_uv",
    )(o_lat, wv_pair)


def _oproj_kernel(op_ref, os_ref, w_ref, h_ref, out_ref, *, n_p):
    o = jnp.where(pl.program_id(0) < n_p, op_ref[...], os_ref[...])
    out_ref[...] = h_ref[...] + _dot(o, w_ref[...])


def _oproj(o_p, o_s, w, h, tm, n_p):
    t, d = h.shape
    hv = o_p.shape[1]
    return pl.pallas_call(
        functools.partial(_oproj_kernel, n_p=n_p),
        out_shape=jax.ShapeDtypeStruct((t, d), F32),
        grid=(t // tm,),
        in_specs=[pl.BlockSpec((tm, hv), lambda i: (jnp.minimum(i, n_p - 1), 0)),
                  pl.BlockSpec((tm, hv), lambda i: (jnp.maximum(i - n_p, 0), 0)),
                  _const_spec(w.shape),
                  pl.BlockSpec((tm, d), lambda i: (i, 0))],
        out_specs=pl.BlockSpec((tm, d), lambda i: (i, 0)),
        compiler_params=_cparams(("parallel",)),
        name="attn_oproj",
    )(o_p, o_s, w, h)


def _rot_cols(w):
    half = w.shape[-1] // 2
    return jnp.concatenate([-w[..., half:], w[..., :half]], axis=-1)


def _moe_schedule(meta, cnt, n_exp, nt_max):
    counts = cnt[0, :n_exp].astype(jnp.int32)
    ntile = (counts + MOE_TILE - 1) // MOE_TILE
    tile_end = jnp.cumsum(ntile)
    seg_start = (tile_end - ntile) * MOE_TILE
    n_tiles = tile_end[-1]
    e = meta[:, 0:TOP_K].astype(jnp.int32)
    rank = meta[:, TOP_K:2 * TOP_K].astype(jnp.int32)
    pos = (seg_start[e] + rank).reshape(-1)
    tid = jnp.minimum(jnp.arange(nt_max, dtype=jnp.int32), n_tiles - 1)
    tile_e = jnp.sum((tid[:, None] >= tile_end[None, :]).astype(jnp.int32), axis=1)
    seg_last = jnp.where(ntile > 0, tile_end - 1, -1)
    tail = n_tiles + jnp.arange(n_exp, dtype=jnp.int32)
    tail = jnp.where(tail < nt_max, tail, -1)
    zero_tiles = jnp.concatenate([seg_last, tail]).astype(jnp.int32)
    return pos, tile_e, tid, n_tiles.reshape(1), zero_tiles


def kernel(x_prompt, x_sample, state_conv, cache_ckv, cache_kpe, p_prompt, p_sample, g_mix, g_ffn, w_pw1, b_pw1, w_dw, b_dw, ln_g, ln_b, w_pw2, b_pw2, g_kv, w_dkv, g_ckv, w_uk, w_uv, w_dq, g_q, w_uq, w_o, w_gate, w_up, w_down, w_router, we_gate, we_up, we_down, w_ple, w_ple_gate, g_ple, g_final):
    n_b, seq, d = x_prompt.shape
    n_bs, seq_s, _ = x_sample.shape
    depth = g_mix.shape[0]
    n_a = w_pw1.shape[0]
    taps = w_dw.shape[1]
    past = cache_ckv.shape[1]
    r_kv = cache_ckv.shape[2]
    rope = cache_kpe.shape[2]
    n_heads, nope = w_uk.shape[1], w_uk.shape[2]
    v_dim = w_uv.shape[2]
    n_exp = w_router.shape[2]
    ple = p_prompt.shape[-1]
    tp, ts = n_b * seq, n_bs * seq_s
    t = tp + ts
    tm = ts
    assert tp % tm == 0 and seq % tm == 0 and tm % MOE_TILE == 0
    assert nope == LANES // 2 and v_dim == LANES // 2 and nope + rope <= LANES and taps - 1 <= HALO
    n_p = tp // tm
    n_sblk = seq // tm
    scale = float(nope + rope) ** -0.5 * math.log2(math.e)
    row = lambda a: a.reshape(1, -1)
    bf = lambda a: a.astype(BF16)

    h = jnp.concatenate([x_prompt.reshape(tp, d), x_sample.reshape(ts, d)], axis=0)
    pp = p_prompt.reshape(depth, tp, ple)
    ps = p_sample.reshape(depth, ts, ple)

    posn = jnp.concatenate([jnp.arange(seq), past + (jnp.arange(ts) % seq_s)]).astype(F32)
    inv = ROPE_THETA ** (-jnp.arange(0, rope, 2, dtype=F32) / rope)
    ang = posn[:, None] * inv[None, :]
    ang = jnp.concatenate([ang, ang], axis=-1)
    pad_r = LANES - nope - rope
    cos_t = jnp.concatenate([jnp.ones((posn.shape[0], nope), F32), jnp.cos(ang),
                             jnp.zeros((posn.shape[0], pad_r), F32)], axis=1)
    sin_t = jnp.concatenate([jnp.zeros((posn.shape[0], nope), F32), jnp.sin(ang),
                             jnp.zeros((posn.shape[0], pad_r), F32)], axis=1)

    conv_p, conv_s = [], []
    c_all = kpe_all = k_cat = v_cat = None
    nt_max = (TOP_K * t) // MOE_TILE + n_exp
    rows_pad = nt_max * MOE_TILE

    for i in range(depth):
        if i < n_a:
            a = i
            u = _pw1_glu(h, row(g_mix[i]), bf(w_pw1[a]), row(b_pw1[a]), tm)
            st_pad = jnp.pad(state_conv[a], ((0, 0), (HALO - (taps - 1), 0), (0, 0))).reshape(n_bs * HALO, d)
            h = _conv_block(u, h, st_pad, w_dw[a], row(b_dw[a]), row(ln_g[a]), row(ln_b[a]),
                            bf(w_pw2[a]), row(b_pw2[a]), n_b=n_b, seq=seq, n_bs=n_bs, seq_s=seq_s, tt=tm)
            conv_p.append(jnp.stack([u[(bi + 1) * seq - (taps - 1):(bi + 1) * seq] for bi in range(n_b)]))
            us = u[tp:].reshape(n_bs, seq_s, d)
            conv_s.append(jnp.concatenate([state_conv[a], us], axis=1)[:, -(taps - 1):])
        else:
            b = i - n_a
            wq = w_uq[b]
            zq = jnp.zeros(wq.shape[:2] + (pad_r,), F32)
            wq_cat = jnp.concatenate([wq, zq], axis=-1).reshape(wq.shape[0], n_heads * LANES)
            wq_rot = jnp.concatenate([jnp.zeros(wq.shape[:2] + (nope,), F32), _rot_cols(wq[..., nope:]), zq],
                                     axis=-1).reshape(wq.shape[0], n_heads * LANES)
            q = _queries(h, row(g_mix[i]), bf(w_dq[b]), row(g_q[b]), bf(wq_cat), bf(wq_rot),
                         cos_t, sin_t, scale, tm, n_p, n_sblk)
            o_p = _prompt_attention(q, k_cat, v_cat, n_b, seq, n_heads, tq=tm, tk=tm)
            wk_t = jnp.transpose(w_uk, (1, 2, 0))
            sel = jnp.zeros((LANES, LANES), F32).at[jnp.arange(nope, nope + rope), jnp.arange(nope, nope + rope)].set(1.0)
            wq2f = jnp.concatenate([
                jnp.concatenate([wk_t, jnp.zeros((n_heads, LANES - nope, r_kv), F32)], axis=1),
                jnp.broadcast_to(sel, (n_heads, LANES, LANES))], axis=2)
            qf = _sample_qfeat(q, bf(wq2f), n_p, ts)
            e_place = jnp.zeros((rope, LANES), F32).at[jnp.arange(rope), nope + jnp.arange(rope)].set(1.0)
            o_lat = _sample_attention(qf, cache_ckv, cache_kpe, c_all, kpe_all, bf(e_place), tp, seq_s)
            wv_h = jnp.transpose(w_uv, (1, 0, 2))
            zv = jnp.zeros_like(wv_h)
            wv_pair = jnp.where((jnp.arange(n_heads) % 2 == 0)[:, None, None],
                                jnp.concatenate([wv_h, zv], axis=2), jnp.concatenate([zv, wv_h], axis=2))
            o_s = _sample_uv(o_lat, bf(wv_pair))
            h = _oproj(o_p, o_s, bf(w_o[b].reshape(n_heads * v_dim, d)), h, tm, n_p)

        j = i // 2
        last = i == depth - 1
        if i % 2 == 0:
            h = _ffn_ple(h, row(g_ffn[i]), bf(w_gate[j]), bf(w_up[j]), bf(w_down[j]), pp, ps, i,
                         row(g_ple[i]), bf(w_ple_gate[i]), bf(w_ple[i]), tm, n_p)
        else:
            wr = jnp.pad(w_router[j], ((0, 0), (0, LANES - n_exp)))
            wr_hi = bf(wr)
            wr_lo = bf(wr - wr_hi.astype(F32))
            meta, cnt = _router(h, row(g_ffn[i]), wr_hi, wr_lo, n_exp, tm)
            pos, tile_e, tile_b, n_tiles, zero_tiles = _moe_schedule(meta, cnt, n_exp, nt_max)
            xs = _dispatch(pos, zero_tiles, h, row(g_ffn[i]), rows_pad, tm)
            ys = _moe_ffn(tile_e, tile_b, n_tiles, xs, bf(we_gate), bf(we_up), bf(we_down), j)
            h = _combine_ple(pos, h, meta, ys, pp, ps, i, row(g_ple[i]), bf(w_ple_gate[i]), bf(w_ple[i]),
                             row(g_final), tm, n_p, final=last)
        if last and i % 2 == 0:
            raise NotImplementedError("final norm is fused into the MoE combine of the last layer")

        if i == n_a - 1:
            w_ext = jnp.concatenate([
                w_dkv[:, :r_kv],
                jnp.zeros((d, nope), F32), w_dkv[:, r_kv:], jnp.zeros((d, pad_r), F32),
                jnp.zeros((d, nope), F32), _rot_cols(w_dkv[:, r_kv:]), jnp.zeros((d, pad_r), F32)], axis=1)
            wk_cat = jnp.concatenate([w_uk, jnp.zeros((r_kv, n_heads, LANES - nope), F32)], axis=2)
            wk_cat = wk_cat.reshape(r_kv, n_heads * LANES)
            lane = jnp.arange(LANES)
            e_k = ((lane[:, None] == (jnp.arange(n_heads * LANES) % LANES)[None, :])
                   & (lane[:, None] >= nope) & (lane[:, None] < nope + rope)).astype(F32)
            zv = jnp.zeros_like(w_uv)
            wv_cat = jnp.where((jnp.arange(n_heads) % 2 == 0)[None, :, None],
                               jnp.concatenate([w_uv, zv], axis=2), jnp.concatenate([zv, w_uv], axis=2))
            wv_cat = wv_cat.reshape(r_kv, n_heads * LANES)
            col = jnp.arange(n_heads * LANES)
            ones_lane = jnp.where((col // LANES) % 2 == 0, v_dim, 0)
            v_ones = (col % LANES == ones_lane).astype(F32).reshape(1, -1)
            c_all, kpe_all, k_cat, v_cat = _latent(h, row(g_kv), bf(w_ext), row(g_ckv), cos_t, sin_t,
                                                   bf(wk_cat), bf(e_k), bf(wv_cat), v_ones, tm, n_p, n_sblk)

    kpe_out = kpe_all[:, nope:nope + rope]
    return (h[:tp].reshape(n_b, seq, d), h[tp:].reshape(n_bs, seq_s, d),
            jnp.stack(conv_p), jnp.stack(conv_s),
            c_all[:tp].reshape(n_b, seq, r_kv), kpe_out[:tp].reshape(n_b, seq, rope),
            c_all[tp:].reshape(n_bs, seq_s, r_kv), kpe_out[tp:].reshape(n_bs, seq_s, rope))
```

```python
import functools
import math

import jax
import jax.numpy as jnp
from jax import lax
from jax.experimental import pallas as pl
from jax.experimental.pallas import tpu as pltpu

EPS = 1e-6
NEG = -1e30
CHUNK = 64
ROPE_THETA = 10000.0
TOP_K = 2

LANES = 128
HALO = 32
MOE_TILE = 256
ISSUE_UNROLL = 8
VMEM_LIMIT = 60 * 1024 * 1024

F32 = jnp.float32
BF16 = jnp.bfloat16


def _cparams(sem=None):
    return pltpu.CompilerParams(dimension_semantics=sem, vmem_limit_bytes=VMEM_LIMIT)


def _const_spec(shape):
    nd = len(shape)
    return pl.BlockSpec(shape, lambda *_: (0,) * nd, pipeline_mode=pl.Buffered(1))


def _rms(x, g):
    return x * lax.rsqrt(jnp.mean(x * x, axis=-1, keepdims=True) + EPS) * g


def _dot(a, b):
    return jnp.dot(a, b, preferred_element_type=F32)


def _dot_nt(a, b):
    return lax.dot_general(a, b, (((1,), (1,)), ((), ())), preferred_element_type=F32)


def _sigmoid(x):
    return 1.0 / (1.0 + jnp.exp(-x))


def _silu(x):
    return x * _sigmoid(x)


def _ff_chunk(f):
    best = None
    for c in range(LANES, min(f, 1536) + 1, LANES):
        if f % c == 0:
            best = c
    return best if best is not None else f


def _pw1_kernel(h_ref, g_ref, w_ref, b_ref, u_ref):
    d = h_ref.shape[1]
    xn = _rms(h_ref[...], g_ref[...]).astype(BF16)
    a = _dot(xn, w_ref[...]) + b_ref[...]
    u_ref[...] = a[:, :d] * _sigmoid(a[:, d:])


def _pw1_glu(h, g, w, b, tm):
    t, d = h.shape
    return pl.pallas_call(
        _pw1_kernel,
        out_shape=jax.ShapeDtypeStruct((t, d), F32),
        grid=(t // tm,),
        in_specs=[pl.BlockSpec((tm, d), lambda i: (i, 0)),
                  _const_spec((1, d)), _const_spec((d, 2 * d)), _const_spec((1, 2 * d))],
        out_specs=pl.BlockSpec((tm, d), lambda i: (i, 0)),
        compiler_params=_cparams(("parallel",)),
        name="pw1_glu",
    )(h, g, w, b)


def _conv_kernel(cur_ref, prev_ref, h_ref, wdw_ref, bdw_ref, lg_ref, lb_ref, w2_ref, b2_ref,
                 o_ref, full_ref, sh_ref, y_ref, *, taps, zero_first):
    tt, d = cur_ref.shape
    nch = d // LANES
    prev = prev_ref[...]
    if zero_first:
        prev = jnp.where(pl.program_id(1) == 0, 0.0, prev)
    for c in range(nch):
        full_ref[c, 0:HALO, :] = prev[:, c * LANES:(c + 1) * LANES]
        full_ref[c, HALO:HALO + tt, :] = cur_ref[:, c * LANES:(c + 1) * LANES]
    off = HALO - (taps - 1)
    n_sh = sh_ref.shape[1]
    rc = min(tt, 32)

    def chunk(c, carry):
        for s in range(1, 8):
            sh_ref[s - 1] = full_ref[c, pl.ds(s, n_sh), :]
        wk = [wdw_ref[c, k:k + 1, :] for k in range(taps)]
        for r in range(tt // rc):
            acc = None
            for k in range(taps):
                j, s = divmod(k + off, 8)
                rows = pl.ds(r * rc + 8 * j, rc)
                term = wk[k] * (full_ref[c, rows, :] if s == 0 else sh_ref[s - 1, rows, :])
                acc = term if acc is None else acc + term
            y_ref[c, r * rc:(r + 1) * rc, :] = acc
        return carry

    lax.fori_loop(0, nch, chunk, 0)
    y = jnp.concatenate([y_ref[c] for c in range(nch)], axis=1) + bdw_ref[...]
    mu = jnp.mean(y, axis=-1, keepdims=True)
    yc = y - mu
    var = jnp.mean(yc * yc, axis=-1, keepdims=True)
    z = _silu(yc * lax.rsqrt(var + EPS) * lg_ref[...] + lb_ref[...])
    o_ref[...] = h_ref[...] + _dot(z.astype(BF16), w2_ref[...]) + b2_ref[...]


def _conv_block(u, h, st_pad, wdw, bdw, lg, lb, w2, b2, *, n_b, seq, n_bs, seq_s, tt):
    t, d = h.shape
    tp = n_b * seq
    taps = wdw.shape[0]
    nblk = seq // tt
    nch = d // LANES
    wdw = jnp.transpose(wdw.reshape(taps, nch, LANES), (1, 0, 2))
    consts = [_const_spec(wdw.shape), _const_spec((1, d)), _const_spec((1, d)), _const_spec((1, d)),
              _const_spec((d, d)), _const_spec((1, d))]
    scratch = lambda rows: [pltpu.VMEM((nch, HALO + rows, LANES), F32),
                            pltpu.VMEM((7, HALO + rows - 8, LANES), F32),
                            pltpu.VMEM((nch, rows, LANES), F32)]
    cur_p = lambda b, i: (b * nblk + i, 0)
    h = pl.pallas_call(
        functools.partial(_conv_kernel, taps=taps, zero_first=True),
        out_shape=jax.ShapeDtypeStruct((t, d), F32),
        grid=(n_b, nblk),
        in_specs=[pl.BlockSpec((tt, d), cur_p),
                  pl.BlockSpec((HALO, d), lambda b, i: (jnp.maximum((b * seq + i * tt) // HALO - 1, 0), 0)),
                  pl.BlockSpec((tt, d), cur_p)] + consts,
        out_specs=pl.BlockSpec((tt, d), cur_p),
        scratch_shapes=scratch(tt),
        input_output_aliases={2: 0},
        compiler_params=_cparams(("parallel", "parallel")),
        name="conv_prompt",
    )(u, u, h, wdw, bdw, lg, lb, w2, b2)
    cur_s = lambda b: (tp // seq_s + b, 0)
    return pl.pallas_call(
        functools.partial(_conv_kernel, taps=taps, zero_first=False),
        out_shape=jax.ShapeDtypeStruct((t, d), F32),
        grid=(n_bs,),
        in_specs=[pl.BlockSpec((seq_s, d), cur_s),
                  pl.BlockSpec((HALO, d), lambda b: (b, 0)),
                  pl.BlockSpec((seq_s, d), cur_s)] + consts,
        out_specs=pl.BlockSpec((seq_s, d), cur_s),
        scratch_shapes=scratch(seq_s),
        input_output_aliases={2: 0},
        compiler_params=_cparams(("parallel",)),
        name="conv_sample",
    )(u, st_pad, h, wdw, bdw, lg, lb, w2, b2)


def _ple(h, p, gp, wpg, wple):
    gate = _sigmoid(_dot(_rms(h, gp).astype(BF16), wpg))
    return h + _dot(p.astype(BF16), wple) * gate


def _p_specs(layer, tm, n_p, ple):
    return [pl.BlockSpec((None, tm, ple), lambda i, *_: (layer, jnp.minimum(i, n_p - 1), 0)),
            pl.BlockSpec((None, tm, ple), lambda i, *_: (layer, jnp.maximum(i - n_p, 0), 0))]


def _ffn_kernel(h_ref, g_ref, wg_ref, wu_ref, wd_ref, pp_ref, ps_ref, gp_ref, wpg_ref, wple_ref,
                o_ref, act_ref, *, n_p, fc):
    h = h_ref[...]
    xn = _rms(h, g_ref[...]).astype(BF16)
    f = wg_ref.shape[1]
    for c in range(f // fc):
        cols = slice(c * fc, (c + 1) * fc)
        act_ref[:, cols] = (_silu(_dot(xn, wg_ref[:, cols])) * _dot(xn, wu_ref[:, cols])).astype(BF16)
    h = h + _dot(act_ref[...], wd_ref[...])
    p = jnp.where(pl.program_id(0) < n_p, pp_ref[...], ps_ref[...])
    o_ref[...] = _ple(h, p, gp_ref[...], wpg_ref[...], wple_ref[...])


def _ffn_ple(h, g, wg, wu, wd, pp, ps, layer, gp, wpg, wple, tm, n_p):
    t, d = h.shape
    f = wg.shape[1]
    ple = pp.shape[-1]
    return pl.pallas_call(
        functools.partial(_ffn_kernel, n_p=n_p, fc=_ff_chunk(f)),
        out_shape=jax.ShapeDtypeStruct((t, d), F32),
        grid=(t // tm,),
        in_specs=[pl.BlockSpec((tm, d), lambda i: (i, 0)), _const_spec((1, d)),
                  _const_spec((d, f)), _const_spec((d, f)), _const_spec((f, d))]
                 + _p_specs(layer, tm, n_p, ple)
                 + [_const_spec((1, d)), _const_spec((d, d)), _const_spec((ple, d))],
        out_specs=pl.BlockSpec((tm, d), lambda i: (i, 0)),
        scratch_shapes=[pltpu.VMEM((tm, f), BF16)],
        compiler_params=_cparams(("parallel",)),
        name="ffn_ple",
    )(h, g, wg, wu, wd, pp, ps, gp, wpg, wple)


def _router_kernel(h_ref, g_ref, whi_ref, wlo_ref, meta_ref, cnt_ref, run_ref, *, n_exp):
    i = pl.program_id(0)

    @pl.when(i == 0)
    def _():
        run_ref[...] = jnp.zeros_like(run_ref)

    tm = h_ref.shape[0]
    xn = _rms(h_ref[...], g_ref[...])
    hi = xn.astype(BF16)
    lo = (xn - hi.astype(F32)).astype(BF16)
    logits = _dot(hi, whi_ref[...]) + (_dot(hi, wlo_ref[...]) + _dot(lo, whi_ref[...]))
    lane = lax.broadcasted_iota(jnp.int32, (tm, LANES), 1).astype(F32)
    logits = jnp.where(lane < n_exp, logits, -jnp.inf)
    m1 = jnp.max(logits, axis=-1, keepdims=True)
    i1 = jnp.min(jnp.where(logits == m1, lane, float(LANES)), axis=-1, keepdims=True)
    rest = jnp.where(lane == i1, -jnp.inf, logits)
    m2 = jnp.max(rest, axis=-1, keepdims=True)
    i2 = jnp.min(jnp.where(rest == m2, lane, float(LANES)), axis=-1, keepdims=True)
    e2 = jnp.exp(m2 - m1)
    w1 = 1.0 / (1.0 + e2)
    w2 = e2 / (1.0 + e2)
    oh1 = (lane == i1).astype(F32)
    oh2 = (lane == i2).astype(F32)
    oh = oh1 + oh2
    rows = lax.broadcasted_iota(jnp.int32, (tm, tm), 0)
    cols = lax.broadcasted_iota(jnp.int32, (tm, tm), 1)
    tri = jnp.where(cols < rows, 1.0, 0.0).astype(BF16)
    pre = _dot(tri, oh.astype(BF16)) + run_ref[...]
    r1 = jnp.sum(pre * oh1, axis=-1, keepdims=True)
    r2 = jnp.sum(pre * oh2, axis=-1, keepdims=True)
    run = run_ref[...] + jnp.sum(oh, axis=0, keepdims=True)
    run_ref[...] = run
    cnt_ref[...] = jnp.broadcast_to(run, cnt_ref.shape)
    meta = jnp.where(lane == 0, i1, 0.0)
    meta = jnp.where(lane == 1, i2, meta)
    meta = jnp.where(lane == 2, r1, meta)
    meta = jnp.where(lane == 3, r2, meta)
    meta = jnp.where(lane == 4, w1, meta)
    meta = jnp.where(lane == 5, w2, meta)
    meta_ref[...] = meta


def _router(h, g, whi, wlo, n_exp, tm):
    t, d = h.shape
    return pl.pallas_call(
        functools.partial(_router_kernel, n_exp=n_exp),
        out_shape=(jax.ShapeDtypeStruct((t, LANES), F32), jax.ShapeDtypeStruct((8, LANES), F32)),
        grid=(t // tm,),
        in_specs=[pl.BlockSpec((tm, d), lambda i: (i, 0)), _const_spec((1, d)),
                  _const_spec((d, LANES)), _const_spec((d, LANES))],
        out_specs=(pl.BlockSpec((tm, LANES), lambda i: (i, 0)), pl.BlockSpec((8, LANES), lambda i: (0, 0))),
        scratch_shapes=[pltpu.VMEM((1, LANES), F32)],
        compiler_params=_cparams(("arbitrary",)),
        name="router",
    )(h, g, whi, wlo)


def _dispatch_kernel(pos_ref, lt_ref, h_ref, g_ref, xs_ref, xn_ref, z_ref, sem, zsem, *, n_exp):
    i = pl.program_id(0)
    tm = h_ref.shape[0]

    @pl.when(i == 0)
    def _():
        z_ref[...] = jnp.zeros_like(z_ref)
        for e in range(2 * n_exp):
            @pl.when(lt_ref[e] >= 0)
            def _():
                start = pl.multiple_of(lt_ref[e] * MOE_TILE, MOE_TILE)
                cp = pltpu.make_async_copy(z_ref, xs_ref.at[pl.ds(start, MOE_TILE)], zsem)
                cp.start()
                cp.wait()

    xn_ref[...] = _rms(h_ref[...], g_ref[...])

    def issue(t, carry):
        for k in range(TOP_K):
            p = pos_ref[(i * tm + t) * TOP_K + k]
            pltpu.make_async_copy(xn_ref.at[pl.ds(t, 1)], xs_ref.at[pl.ds(p, 1)], sem).start()
        return carry

    lax.fori_loop(0, tm, issue, 0, unroll=ISSUE_UNROLL)
    for k in range(TOP_K):
        pltpu.make_async_copy(xn_ref, xs_ref.at[pl.ds(0, tm)], sem).wait()


def _dispatch(pos, last_tile, h, g, rows, tm):
    t, d = h.shape
    return pl.pallas_call(
        functools.partial(_dispatch_kernel, n_exp=last_tile.shape[0] // 2),
        out_shape=jax.ShapeDtypeStruct((rows, d), F32),
        grid_spec=pltpu.PrefetchScalarGridSpec(
            num_scalar_prefetch=2,
            grid=(t // tm,),
            in_specs=[pl.BlockSpec((tm, d), lambda i, pos, lt: (i, 0)),
                      pl.BlockSpec((1, d), lambda i, pos, lt: (0, 0))],
            out_specs=pl.BlockSpec(memory_space=pl.ANY),
            scratch_shapes=[pltpu.VMEM((tm, d), F32), pltpu.VMEM((MOE_TILE, d), F32),
                            pltpu.SemaphoreType.DMA(()), pltpu.SemaphoreType.DMA(())]),
        compiler_params=_cparams(("arbitrary",)),
        name="moe_dispatch",
    )(pos, last_tile, h, g)


def _moe_kernel(te_ref, tb_ref, nt_ref, x_ref, wg_ref, wu_ref, wd_ref, y_ref, act_ref, *, fc):
    @pl.when(pl.program_id(0) >= nt_ref[0])
    def _():
        y_ref[...] = jnp.zeros_like(y_ref)

    @pl.when(pl.program_id(0) < nt_ref[0])
    def _():
        x = x_ref[...].astype(BF16)
        f = wg_ref.shape[1]
        for c in range(f // fc):
            cols = slice(c * fc, (c + 1) * fc)
            act_ref[:, cols] = (_silu(_dot(x, wg_ref[:, cols])) * _dot(x, wu_ref[:, cols])).astype(BF16)
        y_ref[...] = _dot(act_ref[...], wd_ref[...])


def _moe_ffn(tile_e, tile_b, n_tiles, xs, weg, weu, wed, layer):
    rows, d = xs.shape
    f = weg.shape[-1]
    nt_max = tile_e.shape[0]
    wmap = lambda i, te, tb, nt: (layer, te[i], 0, 0)
    xmap = lambda i, te, tb, nt: (tb[i], 0)
    return pl.pallas_call(
        functools.partial(_moe_kernel, fc=_ff_chunk(f)),
        out_shape=jax.ShapeDtypeStruct((rows, d), F32),
        grid_spec=pltpu.PrefetchScalarGridSpec(
            num_scalar_prefetch=3,
            grid=(nt_max,),
            in_specs=[pl.BlockSpec((MOE_TILE, d), xmap),
                      pl.BlockSpec((None, None, d, f), wmap),
                      pl.BlockSpec((None, None, d, f), wmap),
                      pl.BlockSpec((None, None, f, d), wmap)],
            out_specs=pl.BlockSpec((MOE_TILE, d), lambda i, te, tb, nt: (i, 0)),
            scratch_shapes=[pltpu.VMEM((MOE_TILE, f), BF16)]),
        compiler_params=_cparams(("arbitrary",)),
        name="moe_ffn",
    )(tile_e, tile_b, n_tiles, xs, weg, weu, wed)


def _combine_kernel(pos_ref, h_ref, meta_ref, ys_ref, pp_ref, ps_ref, gp_ref, wpg_ref, wple_ref, gf_ref,
                    *rest, n_p, final):
    yk_ref, sem = rest[-2:]
    i = pl.program_id(0)
    tm = h_ref.shape[0]
    slot = i % 2

    def gather(tile, dst_slot):
        def issue(t, carry):
            for k in range(TOP_K):
                p = pos_ref[(tile * tm + t) * TOP_K + k]
                pltpu.make_async_copy(ys_ref.at[pl.ds(p, 1)], yk_ref.at[dst_slot, k, pl.ds(t, 1)],
                                      sem.at[dst_slot]).start()
            return carry

        lax.fori_loop(0, tm, issue, 0, unroll=ISSUE_UNROLL)

    @pl.when(i == 0)
    def _():
        gather(0, 0)

    @pl.when(i + 1 < pl.num_programs(0))
    def _():
        gather(i + 1, 1 - slot)

    for k in range(TOP_K):
        pltpu.make_async_copy(ys_ref.at[pl.ds(0, tm)], yk_ref.at[slot, k], sem.at[slot]).wait()
    meta = meta_ref[...]
    h = h_ref[...] + (meta[:, 4:5] * yk_ref[slot, 0] + meta[:, 5:6] * yk_ref[slot, 1])
    p = jnp.where(i < n_p, pp_ref[...], ps_ref[...])
    h = _ple(h, p, gp_ref[...], wpg_ref[...], wple_ref[...])
    if final:
        h = _rms(h, gf_ref[...])
        op_ref, os_ref = rest[:2]

        @pl.when(i < n_p)
        def _():
            op_ref[...] = h

        @pl.when(i >= n_p)
        def _():
            os_ref[...] = h
    else:
        rest[0][...] = h


def _combine_ple(pos, h, meta, ys, pp, ps, layer, gp, wpg, wple, gf, tm, n_p, final):
    t, d = h.shape
    if final:
        out_shape = (jax.ShapeDtypeStruct((n_p * tm, d), F32), jax.ShapeDtypeStruct((t - n_p * tm, d), F32))
        out_specs = (pl.BlockSpec((tm, d), lambda i, pos: (jnp.minimum(i, n_p - 1), 0)),
                     pl.BlockSpec((tm, d), lambda i, pos: (jnp.maximum(i - n_p, 0), 0)))
    else:
        out_shape = jax.ShapeDtypeStruct((t, d), F32)
        out_specs = pl.BlockSpec((tm, d), lambda i, pos: (i, 0))
    ple = pp.shape[-1]
    c2 = lambda shape: pl.BlockSpec(shape, lambda i, pos: (0, 0))
    return pl.pallas_call(
        functools.partial(_combine_kernel, n_p=n_p, final=final),
        out_shape=out_shape,
        grid_spec=pltpu.PrefetchScalarGridSpec(
            num_scalar_prefetch=1,
            grid=(t // tm,),
            in_specs=[pl.BlockSpec((tm, d), lambda i, pos: (i, 0)),
                      pl.BlockSpec((tm, LANES), lambda i, pos: (i, 0)),
                      pl.BlockSpec(memory_space=pl.ANY)]
                     + _p_specs(layer, tm, n_p, ple)
                     + [c2((1, d)), c2((d, d)), c2((ple, d)), c2((1, d))],
            out_specs=out_specs,
            scratch_shapes=[pltpu.VMEM((2, TOP_K, tm, d), F32), pltpu.SemaphoreType.DMA((2,))]),
        compiler_params=_cparams(("arbitrary",)),
        name="moe_combine",
    )(pos, h, meta, ys, pp, ps, gp, wpg, wple, gf)


def _latent_kernel(h_ref, g_ref, w_ref, gc_ref, cos_ref, sin_ref, wk_ref, ek_ref, wv_ref, one_ref,
                   c_ref, kpe_ref, k_ref, v_ref):
    r = c_ref.shape[1]
    xn = _rms(h_ref[...], g_ref[...]).astype(BF16)
    z = _dot(xn, w_ref[...])
    c = _rms(z[:, :r], gc_ref[...])
    kpe = z[:, r:r + LANES] * cos_ref[...] + z[:, r + LANES:r + 2 * LANES] * sin_ref[...]
    c_ref[...] = c
    kpe_ref[...] = kpe
    cb = c.astype(BF16)
    k_ref[...] = (_dot(cb, wk_ref[...]) + _dot(kpe.astype(BF16), ek_ref[...])).astype(BF16)
    v_ref[...] = (_dot(cb, wv_ref[...]) + one_ref[...]).astype(BF16)


def _tbl_spec(tm, n_p, n_sblk):
    return pl.BlockSpec((tm, LANES), lambda i: (jnp.where(i < n_p, i % n_sblk, n_sblk + i - n_p), 0))


def _latent(h, g, w, gc, cos_t, sin_t, wk, ek, wv, ones, tm, n_p, n_sblk):
    t, d = h.shape
    r = gc.shape[1]
    hk = wk.shape[1]
    row = lambda w_: pl.BlockSpec((tm, w_), lambda i: (i, 0))
    return pl.pallas_call(
        _latent_kernel,
        out_shape=(jax.ShapeDtypeStruct((t, r), F32), jax.ShapeDtypeStruct((t, LANES), F32),
                   jax.ShapeDtypeStruct((t, hk), BF16), jax.ShapeDtypeStruct((t, hk), BF16)),
        grid=(t // tm,),
        in_specs=[row(d), _const_spec((1, d)), _const_spec(w.shape), _const_spec((1, r)),
                  _tbl_spec(tm, n_p, n_sblk), _tbl_spec(tm, n_p, n_sblk),
                  _const_spec(wk.shape), _const_spec(ek.shape), _const_spec(wv.shape), _const_spec(ones.shape)],
        out_specs=(row(r), row(LANES), row(hk), row(hk)),
        compiler_params=_cparams(("parallel",)),
        name="mla_latent",
    )(h, g, w, gc, cos_t, sin_t, wk, ek, wv, ones)


def _query_kernel(h_ref, g_ref, wdq_ref, gq_ref, wq_ref, wqr_ref, cos_ref, sin_ref, q_ref, *, scale):
    xn = _rms(h_ref[...], g_ref[...]).astype(BF16)
    cq = _rms(_dot(xn, wdq_ref[...]), gq_ref[...]).astype(BF16)
    q = _dot(cq, wq_ref[...])
    qr = _dot(cq, wqr_ref[...])
    cos = cos_ref[...] * scale
    sin = sin_ref[...] * scale
    for hd in range(q.shape[1] // LANES):
        lanes = slice(hd * LANES, (hd + 1) * LANES)
        q_ref[:, lanes] = (q[:, lanes] * cos + qr[:, lanes] * sin).astype(BF16)


def _queries(h, g, wdq, gq, wq, wqr, cos_t, sin_t, scale, tm, n_p, n_sblk):
    t, d = h.shape
    hk = wq.shape[1]
    return pl.pallas_call(
        functools.partial(_query_kernel, scale=scale),
        out_shape=jax.ShapeDtypeStruct((t, hk), BF16),
        grid=(t // tm,),
        in_specs=[pl.BlockSpec((tm, d), lambda i: (i, 0)), _const_spec((1, d)), _const_spec(wdq.shape),
                  _const_spec(gq.shape), _const_spec(wq.shape), _const_spec(wqr.shape),
                  _tbl_spec(tm, n_p, n_sblk), _tbl_spec(tm, n_p, n_sblk)],
        out_specs=pl.BlockSpec((tm, hk), lambda i: (i, 0)),
        compiler_params=_cparams(("parallel",)),
        name="mla_queries",
    )(h, g, wdq, gq, wq, wqr, cos_t, sin_t)


def _attn_kernel(q_ref, k_ref, v_ref, o_ref, s00, s01, s10, s11, acc0, acc1, m0, m1, *, tq, tk):
    qi = pl.program_id(2)
    half = LANES // 2
    s_refs = ((s00, s01), (s10, s11))
    acc_refs = (acc0, acc1)
    m_refs = (m0, m1)
    lanes = (slice(0, LANES), slice(LANES, 2 * LANES))
    shift = CHUNK.bit_length() - 1

    def produce(hd, slot, j):
        start = pl.multiple_of(j * tk, tk)
        s_refs[hd][slot][...] = _dot_nt(q_ref[:, lanes[hd]], k_ref[pl.ds(start, tk), lanes[hd]])

    def consume(hd, slot, j, masked):
        start = pl.multiple_of(j * tk, tk)
        s = s_refs[hd][slot][...]
        if masked:
            qpos = qi * tq + lax.broadcasted_iota(jnp.int32, (tq, tk), 0)
            kpos = j * tk + lax.broadcasted_iota(jnp.int32, (tq, tk), 1)
            s = jnp.where(lax.shift_right_logical(kpos, shift) <= lax.shift_right_logical(qpos, shift), s, NEG)
        m_old = m_refs[hd][...]
        m_new = jnp.maximum(m_old, jnp.broadcast_to(jnp.max(s, axis=-1, keepdims=True), m_old.shape))
        p = jnp.exp2(s - jnp.concatenate([m_new] * (tk // LANES), axis=1)).astype(BF16)
        acc_refs[hd][...] = (jnp.exp2(m_old - m_new) * acc_refs[hd][...]
                             + _dot(p, v_ref[pl.ds(start, tk), lanes[hd]]))
        m_refs[hd][...] = m_new

    for hd in range(2):
        m_refs[hd][...] = jnp.full(m_refs[hd].shape, NEG, F32)
        acc_refs[hd][...] = jnp.zeros(acc_refs[hd].shape, F32)
        produce(hd, 0, 0)

    assert tq == tk

    def pair(jj, carry):
        a = 2 * jj
        for hd in range(2):
            produce(hd, 1, a + 1)
        for hd in range(2):
            consume(hd, 0, a, False)
        for hd in range(2):
            produce(hd, 0, a + 2)
        for hd in range(2):
            consume(hd, 1, a + 1, False)
        return carry

    lax.fori_loop(0, qi // 2, pair, 0)
    a = 2 * (qi // 2)

    @pl.when(qi % 2 == 0)
    def _():
        for hd in range(2):
            consume(hd, 0, a, True)

    @pl.when(qi % 2 == 1)
    def _():
        for hd in range(2):
            produce(hd, 1, a + 1)
        for hd in range(2):
            consume(hd, 0, a, False)
        for hd in range(2):
            consume(hd, 1, a + 1, True)

    acc_e, acc_o = acc0[...], acc1[...]
    o_e = acc_e / acc_e[:, half:half + 1]
    o_o = acc_o / acc_o[:, 0:1]
    lane = lax.broadcasted_iota(jnp.int32, (tq, LANES), 1)
    o_ref[...] = jnp.where(lane < half, o_e, o_o).astype(BF16)


def _prompt_attention(q, k, v, n_b, seq, n_heads, tq, tk):
    nq = seq // tq
    hv = n_heads * (LANES // 2)
    return pl.pallas_call(
        functools.partial(_attn_kernel, tq=tq, tk=tk),
        out_shape=jax.ShapeDtypeStruct((n_b * seq, hv), BF16),
        grid=(n_b, n_heads // 2, nq),
        in_specs=[pl.BlockSpec((tq, 2 * LANES), lambda b, hp, i: (b * nq + i, hp)),
                  pl.BlockSpec((seq, 2 * LANES), lambda b, hp, i: (b, hp)),
                  pl.BlockSpec((seq, 2 * LANES), lambda b, hp, i: (b, hp))],
        out_specs=pl.BlockSpec((tq, LANES), lambda b, hp, i: (b * nq + i, hp)),
        scratch_shapes=[pltpu.VMEM((tq, tk), F32)] * 4 + [pltpu.VMEM((tq, LANES), F32)] * 2
                       + [pltpu.VMEM((tq, LANES), F32)] * 2,
        compiler_params=_cparams(("parallel", "parallel", "arbitrary")),
        name="prompt_attention",
    )(q, k, v)


def _qfeat_kernel(q_ref, w_ref, o_ref):
    o_ref[...] = _dot(q_ref[...], w_ref[...]).astype(BF16)


def _sample_qfeat(q, wq2f, row0_blk, ts):
    n_heads, _, fw = wq2f.shape
    return pl.pallas_call(
        _qfeat_kernel,
        out_shape=jax.ShapeDtypeStruct((n_heads, ts, fw), BF16),
        grid=(n_heads,),
        in_specs=[pl.BlockSpec((ts, LANES), lambda hd: (row0_blk, hd)),
                  pl.BlockSpec((None, LANES, fw), lambda hd: (hd, 0, 0))],
        out_specs=pl.BlockSpec((None, ts, fw), lambda hd: (hd, 0, 0)),
        compiler_params=_cparams(("parallel",)),
        name="sample_qfeat",
    )(q, wq2f)


def _sample_attn_kernel(qf_ref, cc_ref, ckpe_ref, cn_ref, kn_ref, e_ref, o_ref):
    n_heads, sq, fw = qf_ref.shape
    r = cc_ref.shape[1]
    qf = qf_ref[...].reshape(n_heads * sq, fw)
    qc, qp = qf[:, :r], qf[:, r:]
    kc = cc_ref[...].astype(BF16)
    kp = _dot(ckpe_ref[...].astype(BF16), e_ref[...]).astype(BF16)
    kcn = cn_ref[...].astype(BF16)
    kpn = kn_ref[...].astype(BF16)
    s1 = _dot_nt(qc, kc) + _dot_nt(qp, kp)
    s2 = _dot_nt(qc, kcn) + _dot_nt(qp, kpn)
    m = jnp.maximum(jnp.max(s1, axis=-1, keepdims=True), jnp.max(s2, axis=-1, keepdims=True))
    p1 = jnp.exp2(s1 - m)
    p2 = jnp.exp2(s2 - m)
    l = jnp.sum(p1, axis=-1, keepdims=True) + jnp.sum(p2, axis=-1, keepdims=True)
    o = (_dot(p1.astype(BF16), kc) + _dot(p2.astype(BF16), kcn)) / l
    o_ref[...] = o.astype(BF16).reshape(n_heads, sq, r)


def _sample_attention(qf, cache_c, cache_kpe, c_all, kpe_all, e_place, tp, sq):
    n_heads, ts, fw = qf.shape
    n_bs, past, r = cache_c.shape
    rope = cache_kpe.shape[-1]
    return pl.pallas_call(
        _sample_attn_kernel,
        out_shape=jax.ShapeDtypeStruct((n_heads, ts, r), BF16),
        grid=(n_bs,),
        in_specs=[pl.BlockSpec((n_heads, sq, fw), lambda b: (0, b, 0)),
                  pl.BlockSpec((None, past, r), lambda b: (b, 0, 0)),
                  pl.BlockSpec((None, past, rope), lambda b: (b, 0, 0)),
                  pl.BlockSpec((sq, r), lambda b: (tp // sq + b, 0)),
                  pl.BlockSpec((sq, LANES), lambda b: (tp // sq + b, 0)),
                  _const_spec(e_place.shape)],
        out_specs=pl.BlockSpec((n_heads, sq, r), lambda b: (0, b, 0)),
        compiler_params=_cparams(("parallel",)),
        name="sample_attention",
    )(qf, cache_c, cache_kpe, c_all, kpe_all, e_place)


def _sample_uv_kernel(ol_ref, wv_ref, o_ref):
    o_ref[...] = (_dot(ol_ref[0], wv_ref[0]) + _dot(ol_ref[1], wv_ref[1])).astype(BF16)


def _sample_uv(o_lat, wv_pair):
    n_heads, ts, r = o_lat.shape
    return pl.pallas_call(
        _sample_uv_kernel,
        out_shape=jax.ShapeDtypeStruct((ts, n_heads * (LANES // 2)), BF16),
        grid=(n_heads // 2,),
        in_specs=[pl.BlockSpec((2, ts, r), lambda hp: (hp, 0, 0)),
                  pl.BlockSpec((2, r, LANES), lambda hp: (hp, 0, 0))],
        out_specs=pl.BlockSpec((ts, LANES), lambda hp: (0, hp)),
        compiler_params=_cparams(("parallel",)),
        name="sample_uv",
    )(o_lat, wv_pair)


def _oproj_kernel(op_ref, os_ref, w_ref, h_ref, out_ref, *, n_p):
    o = jnp.where(pl.program_id(0) < n_p, op_ref[...], os_ref[...])
    out_ref[...] = h_ref[...] + _dot(o, w_ref[...])


def _oproj(o_p, o_s, w, h, tm, n_p):
    t, d = h.shape
    hv = o_p.shape[1]
    return pl.pallas_call(
        functools.partial(_oproj_kernel, n_p=n_p),
        out_shape=jax.ShapeDtypeStruct((t, d), F32),
        grid=(t // tm,),
        in_specs=[pl.BlockSpec((tm, hv), lambda i: (jnp.minimum(i, n_p - 1), 0)),
                  pl.BlockSpec((tm, hv), lambda i: (jnp.maximum(i - n_p, 0), 0)),
                  _const_spec(w.shape),
                  pl.BlockSpec((tm, d), lambda i: (i, 0))],
        out_specs=pl.BlockSpec((tm, d), lambda i: (i, 0)),
        compiler_params=_cparams(("parallel",)),
        name="attn_oproj",
    )(o_p, o_s, w, h)


def _rot_cols(w):
    half = w.shape[-1] // 2
    return jnp.concatenate([-w[..., half:], w[..., :half]], axis=-1)


def _moe_schedule(meta, cnt, n_exp, nt_max):
    counts = cnt[0, :n_exp].astype(jnp.int32)
    ntile = (counts + MOE_TILE - 1) // MOE_TILE
    tile_end = jnp.cumsum(ntile)
    seg_start = (tile_end - ntile) * MOE_TILE
    n_tiles = tile_end[-1]
    e = meta[:, 0:TOP_K].astype(jnp.int32)
    rank = meta[:, TOP_K:2 * TOP_K].astype(jnp.int32)
    pos = (seg_start[e] + rank).reshape(-1)
    tid = jnp.minimum(jnp.arange(nt_max, dtype=jnp.int32), n_tiles - 1)
    tile_e = jnp.sum((tid[:, None] >= tile_end[None, :]).astype(jnp.int32), axis=1)
    seg_last = jnp.where(ntile > 0, tile_end - 1, -1)
    tail = n_tiles + jnp.arange(n_exp, dtype=jnp.int32)
    tail = jnp.where(tail < nt_max, tail, -1)
    zero_tiles = jnp.concatenate([seg_last, tail]).astype(jnp.int32)
    return pos, tile_e, tid, n_tiles.reshape(1), zero_tiles


def kernel(x_prompt, x_sample, state_conv, cache_ckv, cache_kpe, p_prompt, p_sample, g_mix, g_ffn, w_pw1, b_pw1, w_dw, b_dw, ln_g, ln_b, w_pw2, b_pw2, g_kv, w_dkv, g_ckv, w_uk, w_uv, w_dq, g_q, w_uq, w_o, w_gate, w_up, w_down, w_router, we_gate, we_up, we_down, w_ple, w_ple_gate, g_ple, g_final):
    n_b, seq, d = x_prompt.shape
    n_bs, seq_s, _ = x_sample.shape
    depth = g_mix.shape[0]
    n_a = w_pw1.shape[0]
    taps = w_dw.shape[1]
    past = cache_ckv.shape[1]
    r_kv = cache_ckv.shape[2]
    rope = cache_kpe.shape[2]
    n_heads, nope = w_uk.shape[1], w_uk.shape[2]
    v_dim = w_uv.shape[2]
    n_exp = w_router.shape[2]
    ple = p_prompt.shape[-1]
    tp, ts = n_b * seq, n_bs * seq_s
    t = tp + ts
    tm = ts
    assert tp % tm == 0 and seq % tm == 0 and tm % MOE_TILE == 0
    assert nope == LANES // 2 and v_dim == LANES // 2 and nope + rope <= LANES and taps - 1 <= HALO
    n_p = tp // tm
    n_sblk = seq // tm
    scale = float(nope + rope) ** -0.5 * math.log2(math.e)
    row = lambda a: a.reshape(1, -1)
    bf = lambda a: a.astype(BF16)

    h = jnp.concatenate([x_prompt.reshape(tp, d), x_sample.reshape(ts, d)], axis=0)
    pp = p_prompt.reshape(depth, tp, ple)
    ps = p_sample.reshape(depth, ts, ple)

    posn = jnp.concatenate([jnp.arange(seq), past + (jnp.arange(ts) % seq_s)]).astype(F32)
    inv = ROPE_THETA ** (-jnp.arange(0, rope, 2, dtype=F32) / rope)
    ang = posn[:, None] * inv[None, :]
    ang = jnp.concatenate([ang, ang], axis=-1)
    pad_r = LANES - nope - rope
    cos_t = jnp.concatenate([jnp.ones((posn.shape[0], nope), F32), jnp.cos(ang),
                             jnp.zeros((posn.shape[0], pad_r), F32)], axis=1)
    sin_t = jnp.concatenate([jnp.zeros((posn.shape[0], nope), F32), jnp.sin(ang),
                             jnp.zeros((posn.shape[0], pad_r), F32)], axis=1)

    conv_p, conv_s = [], []
    c_all = kpe_all = k_cat = v_cat = None
    nt_max = (TOP_K * t) // MOE_TILE + n_exp
    rows_pad = nt_max * MOE_TILE

    for i in range(depth):
        if i < n_a:
            a = i
            u = _pw1_glu(h, row(g_mix[i]), bf(w_pw1[a]), row(b_pw1[a]), tm)
            st_pad = jnp.pad(state_conv[a], ((0, 0), (HALO - (taps - 1), 0), (0, 0))).reshape(n_bs * HALO, d)
            h = _conv_block(u, h, st_pad, w_dw[a], row(b_dw[a]), row(ln_g[a]), row(ln_b[a]),
                            bf(w_pw2[a]), row(b_pw2[a]), n_b=n_b, seq=seq, n_bs=n_bs, seq_s=seq_s, tt=tm)
            conv_p.append(jnp.stack([u[(bi + 1) * seq - (taps - 1):(bi + 1) * seq] for bi in range(n_b)]))
            us = u[tp:].reshape(n_bs, seq_s, d)
            conv_s.append(jnp.concatenate([state_conv[a], us], axis=1)[:, -(taps - 1):])
        else:
            b = i - n_a
            wq = w_uq[b]
            zq = jnp.zeros(wq.shape[:2] + (pad_r,), F32)
            wq_cat = jnp.concatenate([wq, zq], axis=-1).reshape(wq.shape[0], n_heads * LANES)
            wq_rot = jnp.concatenate([jnp.zeros(wq.shape[:2] + (nope,), F32), _rot_cols(wq[..., nope:]), zq],
                                     axis=-1).reshape(wq.shape[0], n_heads * LANES)
            q = _queries(h, row(g_mix[i]), bf(w_dq[b]), row(g_q[b]), bf(wq_cat), bf(wq_rot),
                         cos_t, sin_t, scale, tm, n_p, n_sblk)
            o_p = _prompt_attention(q, k_cat, v_cat, n_b, seq, n_heads, tq=tm, tk=tm)
            wk_t = jnp.transpose(w_uk, (1, 2, 0))
            sel = jnp.zeros((LANES, LANES), F32).at[jnp.arange(nope, nope + rope), jnp.arange(nope, nope + rope)].set(1.0)
            wq2f = jnp.concatenate([
                jnp.concatenate([wk_t, jnp.zeros((n_heads, LANES - nope, r_kv), F32)], axis=1),
                jnp.broadcast_to(sel, (n_heads, LANES, LANES))], axis=2)
            qf = _sample_qfeat(q, bf(wq2f), n_p, ts)
            e_place = jnp.zeros((rope, LANES), F32).at[jnp.arange(rope), nope + jnp.arange(rope)].set(1.0)
            o_lat = _sample_attention(qf, cache_ckv, cache_kpe, c_all, kpe_all, bf(e_place), tp, seq_s)
            wv_h = jnp.transpose(w_uv, (1, 0, 2))
            zv = jnp.zeros_like(wv_h)
            wv_pair = jnp.where((jnp.arange(n_heads) % 2 == 0)[:, None, None],
                                jnp.concatenate([wv_h, zv], axis=2), jnp.concatenate([zv, wv_h], axis=2))
            o_s = _sample_uv(o_lat, bf(wv_pair))
            h = _oproj(o_p, o_s, bf(w_o[b].reshape(n_heads * v_dim, d)), h, tm, n_p)

        j = i // 2
        last = i == depth - 1
        if i % 2 == 0:
            h = _ffn_ple(h, row(g_ffn[i]), bf(w_gate[j]), bf(w_up[j]), bf(w_down[j]), pp, ps, i,
                         row(g_ple[i]), bf(w_ple_gate[i]), bf(w_ple[i]), tm, n_p)
        else:
            wr = jnp.pad(w_router[j], ((0, 0), (0, LANES - n_exp)))
            wr_hi = bf(wr)
            wr_lo = bf(wr - wr_hi.astype(F32))
            meta, cnt = _router(h, row(g_ffn[i]), wr_hi, wr_lo, n_exp, tm)
            pos, tile_e, tile_b, n_tiles, zero_tiles = _moe_schedule(meta, cnt, n_exp, nt_max)
            xs = _dispatch(pos, zero_tiles, h, row(g_ffn[i]), rows_pad, tm)
            ys = _moe_ffn(tile_e, tile_b, n_tiles, xs, bf(we_gate), bf(we_up), bf(we_down), j)
            h = _combine_ple(pos, h, meta, ys, pp, ps, i, row(g_ple[i]), bf(w_ple_gate[i]), bf(w_ple[i]),
                             row(g_final), tm, n_p, final=last)
        if last and i % 2 == 0:
            raise NotImplementedError("final norm is fused into the MoE combine of the last layer")

        if i == n_a - 1:
            w_ext = jnp.concatenate([
                w_dkv[:, :r_kv],
                jnp.zeros((d, nope), F32), w_dkv[:, r_kv:], jnp.zeros((d, pad_r), F32),
                jnp.zeros((d, nope), F32), _rot_cols(w_dkv[:, r_kv:]), jnp.zeros((d, pad_r), F32)], axis=1)
            wk_cat = jnp.concatenate([w_uk, jnp.zeros((r_kv, n_heads, LANES - nope), F32)], axis=2)
            wk_cat = wk_cat.reshape(r_kv, n_heads * LANES)
            lane = jnp.arange(LANES)
            e_k = ((lane[:, None] == (jnp.arange(n_heads * LANES) % LANES)[None, :])
                   & (lane[:, None] >= nope) & (lane[:, None] < nope + rope)).astype(F32)
            zv = jnp.zeros_like(w_uv)
            wv_cat = jnp.where((jnp.arange(n_heads) % 2 == 0)[None, :, None],
                               jnp.concatenate([w_uv, zv], axis=2), jnp.concatenate([zv, w_uv], axis=2))
            wv_cat = wv_cat.reshape(r_kv, n_heads * LANES)
            col = jnp.arange(n_heads * LANES)
            ones_lane = jnp.where((col // LANES) % 2 == 0, v_dim, 0)
            v_ones = (col % LANES == ones_lane).astype(F32).reshape(1, -1)
            c_all, kpe_all, k_cat, v_cat = _latent(h, row(g_kv), bf(w_ext), row(g_ckv), cos_t, sin_t,
                                                   bf(wk_cat), bf(e_k), bf(wv_cat), v_ones, tm, n_p, n_sblk)

    kpe_out = kpe_all[:, nope:nope + rope]
    y_p, y_s = h
    return (y_p.reshape(n_b, seq, d), y_s.reshape(n_bs, seq_s, d),
            jnp.stack(conv_p), jnp.stack(conv_s),
            c_all[:tp].reshape(n_b, seq, r_kv), kpe_out[:tp].reshape(n_b, seq, rope),
            c_all[tp:].reshape(n_bs, seq_s, r_kv), kpe_out[tp:].reshape(n_bs, seq_s, rope))
```

```python
import functools
import math

import jax
import jax.numpy as jnp
from jax import lax
from jax.experimental import pallas as pl
from jax.experimental.pallas import tpu as pltpu

EPS = 1e-6
NEG = -1e30
CHUNK = 64
ROPE_THETA = 10000.0
TOP_K = 2

LANES = 128
HALO = 32
MOE_TILE = 256
ISSUE_UNROLL = 8
VMEM_LIMIT = 60 * 1024 * 1024

F32 = jnp.float32
BF16 = jnp.bfloat16


def _cparams(sem=None):
    return pltpu.CompilerParams(dimension_semantics=sem, vmem_limit_bytes=VMEM_LIMIT)


def _const_spec(shape):
    nd = len(shape)
    return pl.BlockSpec(shape, lambda *_: (0,) * nd, pipeline_mode=pl.Buffered(1))


def _rms(x, g):
    return x * lax.rsqrt(jnp.mean(x * x, axis=-1, keepdims=True) + EPS) * g


def _dot(a, b):
    return jnp.dot(a, b, preferred_element_type=F32)


def _dot_nt(a, b):
    return lax.dot_general(a, b, (((1,), (1,)), ((), ())), preferred_element_type=F32)


def _sigmoid(x):
    return 1.0 / (1.0 + jnp.exp(-x))


def _silu(x):
    return x * _sigmoid(x)


def _ff_chunk(f):
    best = None
    for c in range(LANES, min(f, 1536) + 1, LANES):
        if f % c == 0:
            best = c
    return best if best is not None else f


def _pw1_kernel(h_ref, g_ref, w_ref, b_ref, u_ref):
    d = h_ref.shape[1]
    xn = _rms(h_ref[...], g_ref[...]).astype(BF16)
    a = _dot(xn, w_ref[...]) + b_ref[...]
    u_ref[...] = a[:, :d] * _sigmoid(a[:, d:])


def _pw1_glu(h, g, w, b, tm):
    t, d = h.shape
    return pl.pallas_call(
        _pw1_kernel,
        out_shape=jax.ShapeDtypeStruct((t, d), F32),
        grid=(t // tm,),
        in_specs=[pl.BlockSpec((tm, d), lambda i: (i, 0)),
                  _const_spec((1, d)), _const_spec((d, 2 * d)), _const_spec((1, 2 * d))],
        out_specs=pl.BlockSpec((tm, d), lambda i: (i, 0)),
        compiler_params=_cparams(("parallel",)),
        name="pw1_glu",
    )(h, g, w, b)


def _conv_kernel(cur_ref, prev_ref, h_ref, wdw_ref, bdw_ref, lg_ref, lb_ref, w2_ref, b2_ref,
                 o_ref, full_ref, sh_ref, y_ref, *, taps, zero_first):
    tt, d = cur_ref.shape
    nch = d // LANES
    prev = prev_ref[...]
    if zero_first:
        prev = jnp.where(pl.program_id(1) == 0, 0.0, prev)
    for c in range(nch):
        full_ref[c, 0:HALO, :] = prev[:, c * LANES:(c + 1) * LANES]
        full_ref[c, HALO:HALO + tt, :] = cur_ref[:, c * LANES:(c + 1) * LANES]
    off = HALO - (taps - 1)
    n_sh = sh_ref.shape[1]
    rc = min(tt, 32)

    def chunk(c, carry):
        for s in range(1, 8):
            sh_ref[s - 1] = full_ref[c, pl.ds(s, n_sh), :]
        wk = [wdw_ref[c, k:k + 1, :] for k in range(taps)]
        for r in range(tt // rc):
            acc = None
            for k in range(taps):
                j, s = divmod(k + off, 8)
                rows = pl.ds(r * rc + 8 * j, rc)
                term = wk[k] * (full_ref[c, rows, :] if s == 0 else sh_ref[s - 1, rows, :])
                acc = term if acc is None else acc + term
            y_ref[c, r * rc:(r + 1) * rc, :] = acc
        return carry

    lax.fori_loop(0, nch, chunk, 0)
    y = jnp.concatenate([y_ref[c] for c in range(nch)], axis=1) + bdw_ref[...]
    mu = jnp.mean(y, axis=-1, keepdims=True)
    yc = y - mu
    var = jnp.mean(yc * yc, axis=-1, keepdims=True)
    z = _silu(yc * lax.rsqrt(var + EPS) * lg_ref[...] + lb_ref[...])
    o_ref[...] = h_ref[...] + _dot(z.astype(BF16), w2_ref[...]) + b2_ref[...]


def _conv_block(u, h, st_pad, wdw, bdw, lg, lb, w2, b2, *, n_b, seq, n_bs, seq_s, tt):
    t, d = h.shape
    tp = n_b * seq
    taps = wdw.shape[0]
    nblk = seq // tt
    nch = d // LANES
    wdw = jnp.transpose(wdw.reshape(taps, nch, LANES), (1, 0, 2))
    consts = [_const_spec(wdw.shape), _const_spec((1, d)), _const_spec((1, d)), _const_spec((1, d)),
              _const_spec((d, d)), _const_spec((1, d))]
    scratch = lambda rows: [pltpu.VMEM((nch, HALO + rows, LANES), F32),
                            pltpu.VMEM((7, HALO + rows - 8, LANES), F32),
                            pltpu.VMEM((nch, rows, LANES), F32)]
    cur_p = lambda b, i: (b * nblk + i, 0)
    h = pl.pallas_call(
        functools.partial(_conv_kernel, taps=taps, zero_first=True),
        out_shape=jax.ShapeDtypeStruct((t, d), F32),
        grid=(n_b, nblk),
        in_specs=[pl.BlockSpec((tt, d), cur_p),
                  pl.BlockSpec((HALO, d), lambda b, i: (jnp.maximum((b * seq + i * tt) // HALO - 1, 0), 0)),
                  pl.BlockSpec((tt, d), cur_p)] + consts,
        out_specs=pl.BlockSpec((tt, d), cur_p),
        scratch_shapes=scratch(tt),
        input_output_aliases={2: 0},
        compiler_params=_cparams(("parallel", "parallel")),
        name="conv_prompt",
    )(u, u, h, wdw, bdw, lg, lb, w2, b2)
    cur_s = lambda b: (tp // seq_s + b, 0)
    return pl.pallas_call(
        functools.partial(_conv_kernel, taps=taps, zero_first=False),
        out_shape=jax.ShapeDtypeStruct((t, d), F32),
        grid=(n_bs,),
        in_specs=[pl.BlockSpec((seq_s, d), cur_s),
                  pl.BlockSpec((HALO, d), lambda b: (b, 0)),
                  pl.BlockSpec((seq_s, d), cur_s)] + consts,
        out_specs=pl.BlockSpec((seq_s, d), cur_s),
        scratch_shapes=scratch(seq_s),
        input_output_aliases={2: 0},
        compiler_params=_cparams(("parallel",)),
        name="conv_sample",
    )(u, st_pad, h, wdw, bdw, lg, lb, w2, b2)


def _ple(h, p, gp, wpg, wple):
    gate = _sigmoid(_dot(_rms(h, gp).astype(BF16), wpg))
    return h + _dot(p.astype(BF16), wple) * gate


def _p_specs(layer, tm, n_p, ple):
    return [pl.BlockSpec((None, tm, ple), lambda i, *_: (layer, jnp.minimum(i, n_p - 1), 0)),
            pl.BlockSpec((None, tm, ple), lambda i, *_: (layer, jnp.maximum(i - n_p, 0), 0))]


def _ffn_kernel(h_ref, g_ref, wg_ref, wu_ref, wd_ref, pp_ref, ps_ref, gp_ref, wpg_ref, wple_ref,
                o_ref, act_ref, *, n_p, fc):
    h = h_ref[...]
    xn = _rms(h, g_ref[...]).astype(BF16)
    f = wg_ref.shape[1]
    for c in range(f // fc):
        cols = slice(c * fc, (c + 1) * fc)
        act_ref[:, cols] = (_silu(_dot(xn, wg_ref[:, cols])) * _dot(xn, wu_ref[:, cols])).astype(BF16)
    h = h + _dot(act_ref[...], wd_ref[...])
    p = jnp.where(pl.program_id(0) < n_p, pp_ref[...], ps_ref[...])
    o_ref[...] = _ple(h, p, gp_ref[...], wpg_ref[...], wple_ref[...])


def _ffn_ple(h, g, wg, wu, wd, pp, ps, layer, gp, wpg, wple, tm, n_p):
    t, d = h.shape
    f = wg.shape[1]
    ple = pp.shape[-1]
    return pl.pallas_call(
        functools.partial(_ffn_kernel, n_p=n_p, fc=_ff_chunk(f)),
        out_shape=jax.ShapeDtypeStruct((t, d), F32),
        grid=(t // tm,),
        in_specs=[pl.BlockSpec((tm, d), lambda i: (i, 0)), _const_spec((1, d)),
                  _const_spec((d, f)), _const_spec((d, f)), _const_spec((f, d))]
                 + _p_specs(layer, tm, n_p, ple)
                 + [_const_spec((1, d)), _const_spec((d, d)), _const_spec((ple, d))],
        out_specs=pl.BlockSpec((tm, d), lambda i: (i, 0)),
        scratch_shapes=[pltpu.VMEM((tm, f), BF16)],
        compiler_params=_cparams(("parallel",)),
        name="ffn_ple",
    )(h, g, wg, wu, wd, pp, ps, gp, wpg, wple)


def _router_kernel(h_ref, g_ref, whi_ref, wlo_ref, meta_ref, cnt_ref, run_ref, *, n_exp):
    i = pl.program_id(0)

    @pl.when(i == 0)
    def _():
        run_ref[...] = jnp.zeros_like(run_ref)

    tm = h_ref.shape[0]
    xn = _rms(h_ref[...], g_ref[...])
    hi = xn.astype(BF16)
    lo = (xn - hi.astype(F32)).astype(BF16)
    logits = _dot(hi, whi_ref[...]) + (_dot(hi, wlo_ref[...]) + _dot(lo, whi_ref[...]))
    lane = lax.broadcasted_iota(jnp.int32, (tm, LANES), 1).astype(F32)
    logits = jnp.where(lane < n_exp, logits, -jnp.inf)
    m1 = jnp.max(logits, axis=-1, keepdims=True)
    i1 = jnp.min(jnp.where(logits == m1, lane, float(LANES)), axis=-1, keepdims=True)
    rest = jnp.where(lane == i1, -jnp.inf, logits)
    m2 = jnp.max(rest, axis=-1, keepdims=True)
    i2 = jnp.min(jnp.where(rest == m2, lane, float(LANES)), axis=-1, keepdims=True)
    e2 = jnp.exp(m2 - m1)
    w1 = 1.0 / (1.0 + e2)
    w2 = e2 / (1.0 + e2)
    oh1 = (lane == i1).astype(F32)
    oh2 = (lane == i2).astype(F32)
    oh = oh1 + oh2
    rows = lax.broadcasted_iota(jnp.int32, (tm, tm), 0)
    cols = lax.broadcasted_iota(jnp.int32, (tm, tm), 1)
    tri = jnp.where(cols < rows, 1.0, 0.0).astype(BF16)
    pre = _dot(tri, oh.astype(BF16)) + run_ref[...]
    r1 = jnp.sum(pre * oh1, axis=-1, keepdims=True)
    r2 = jnp.sum(pre * oh2, axis=-1, keepdims=True)
    run = run_ref[...] + jnp.sum(oh, axis=0, keepdims=True)
    run_ref[...] = run
    cnt_ref[...] = jnp.broadcast_to(run, cnt_ref.shape)
    meta = jnp.where(lane == 0, i1, 0.0)
    meta = jnp.where(lane == 1, i2, meta)
    meta = jnp.where(lane == 2, r1, meta)
    meta = jnp.where(lane == 3, r2, meta)
    meta = jnp.where(lane == 4, w1, meta)
    meta = jnp.where(lane == 5, w2, meta)
    meta_ref[...] = meta


def _router(h, g, whi, wlo, n_exp, tm):
    t, d = h.shape
    return pl.pallas_call(
        functools.partial(_router_kernel, n_exp=n_exp),
        out_shape=(jax.ShapeDtypeStruct((t, LANES), F32), jax.ShapeDtypeStruct((8, LANES), F32)),
        grid=(t // tm,),
        in_specs=[pl.BlockSpec((tm, d), lambda i: (i, 0)), _const_spec((1, d)),
                  _const_spec((d, LANES)), _const_spec((d, LANES))],
        out_specs=(pl.BlockSpec((tm, LANES), lambda i: (i, 0)), pl.BlockSpec((8, LANES), lambda i: (0, 0))),
        scratch_shapes=[pltpu.VMEM((1, LANES), F32)],
        compiler_params=_cparams(("arbitrary",)),
        name="router",
    )(h, g, whi, wlo)


def _inverse_kernel(pos_ref, lo_ref, hi_ref, inv_ref, *, n_rng):
    for e in range(n_rng):
        def fill(r, carry):
            inv_ref[r] = -1
            return carry

        lax.fori_loop(lo_ref[e], hi_ref[e], fill, 0)

    def put(idx, carry):
        inv_ref[pos_ref[idx]] = idx
        return carry

    lax.fori_loop(0, pos_ref.shape[0], put, 0, unroll=ISSUE_UNROLL)


def _moe_inverse(pos, lo, hi, rows):
    return pl.pallas_call(
        functools.partial(_inverse_kernel, n_rng=lo.shape[0]),
        out_shape=jax.ShapeDtypeStruct((rows,), jnp.int32),
        grid_spec=pltpu.PrefetchScalarGridSpec(
            num_scalar_prefetch=3, grid=(1,), in_specs=[],
            out_specs=pl.BlockSpec(memory_space=pltpu.SMEM)),
        compiler_params=_cparams(("arbitrary",)),
        name="moe_inverse",
    )(pos, lo, hi)


def _moe_perm_kernel(te_ref, inv_ref, h_ref, g_ref, wg_ref, wu_ref, wd_ref, o_ref,
                     xg_ref, yb_ref, act_ref, gsem, ssem, zsem, *, fc, t_tok, n_steps):
    i = pl.program_id(0)
    last = n_steps - 1
    trash = TOP_K * t_tok

    def gather_row(tile, slot, r):
        idx = inv_ref[tile * MOE_TILE + r]
        tok = lax.shift_right_logical(jnp.maximum(idx, 0), 1)
        pltpu.make_async_copy(h_ref.at[pl.ds(tok, 1)], xg_ref.at[slot, pl.ds(r, 1)], gsem.at[slot]).start()

    def scatter_row(tile, slot, r):
        idx = inv_ref[tile * MOE_TILE + r]
        dest = jnp.where(idx >= 0, (idx & 1) * t_tok + lax.shift_right_logical(idx, 1), trash + r)
        pltpu.make_async_copy(yb_ref.at[slot, pl.ds(r, 1)], o_ref.at[pl.ds(dest, 1)], ssem.at[slot]).start()

    def rolled(row_fn, tile, slot):
        def body(r, carry):
            row_fn(tile, slot, r)
            return carry

        lax.fori_loop(0, MOE_TILE, body, 0, unroll=ISSUE_UNROLL)

    def wait_gather(slot):
        pltpu.make_async_copy(h_ref.at[pl.ds(0, MOE_TILE)], xg_ref.at[slot], gsem.at[slot]).wait()

    def wait_scatter(slot):
        pltpu.make_async_copy(yb_ref.at[slot], o_ref.at[pl.ds(0, MOE_TILE)], ssem.at[slot]).wait()

    def compute(slot, issue_rows):
        f = wg_ref.shape[1]
        n_stage = f // fc + 1
        share = MOE_TILE // n_stage
        x = _rms(xg_ref[slot], g_ref[...]).astype(BF16)
        for c in range(f // fc):
            issue_rows(c * share, (c + 1) * share)
            cols = slice(c * fc, (c + 1) * fc)
            act_ref[:, cols] = (_silu(_dot(x, wg_ref[:, cols])) * _dot(x, wu_ref[:, cols])).astype(BF16)
        issue_rows((n_stage - 1) * share, MOE_TILE)
        yb_ref[slot] = _dot(act_ref[...], wd_ref[...])

    @pl.when(i == 0)
    def _():
        yb_ref[1] = jnp.zeros(yb_ref.shape[1:], F32)
        cp = pltpu.make_async_copy(yb_ref.at[1], o_ref.at[pl.ds(trash, MOE_TILE)], zsem)
        cp.start()
        cp.wait()
        rolled(gather_row, 0, 0)
        wait_gather(0)
        def issue_rows(r0, r1):
            for r in range(r0, r1):
                gather_row(1, 1, r)

        compute(0, issue_rows)

    for par in range(2):
        @pl.when((i > 0) & (i % 2 == par))
        def _():
            wait_gather(par)

            @pl.when(i >= 2)
            def _():
                wait_scatter(par)

            nxt = jnp.minimum(i + 1, last)

            def issue_rows(r0, r1):
                for r in range(r0, r1):
                    gather_row(nxt, 1 - par, r)
                    scatter_row(i - 1, 1 - par, r)

            compute(par, issue_rows)

    @pl.when(i == last)
    def _():
        wait_scatter(1 - last % 2)
        rolled(scatter_row, last, last % 2)
        wait_scatter(last % 2)
        wait_gather(1 - last % 2)


def _moe_perm_ffn(tile_e, inv, h, g, weg, weu, wed, layer):
    t, d = h.shape
    f = weg.shape[-1]
    n_steps = tile_e.shape[0]
    wmap = lambda i, te, inv_: (layer, te[i], 0, 0)
    buf = lambda: pltpu.VMEM((2, MOE_TILE, d), F32)
    return pl.pallas_call(
        functools.partial(_moe_perm_kernel, fc=_ff_chunk(f), t_tok=t, n_steps=n_steps),
        out_shape=jax.ShapeDtypeStruct((TOP_K * t + MOE_TILE, d), F32),
        grid_spec=pltpu.PrefetchScalarGridSpec(
            num_scalar_prefetch=2,
            grid=(n_steps,),
            in_specs=[pl.BlockSpec(memory_space=pl.ANY),
                      pl.BlockSpec((1, d), lambda i, te, inv_: (0, 0)),
                      pl.BlockSpec((None, None, d, f), wmap),
                      pl.BlockSpec((None, None, d, f), wmap),
                      pl.BlockSpec((None, None, f, d), wmap)],
            out_specs=pl.BlockSpec(memory_space=pl.ANY),
            scratch_shapes=[buf(), buf(), pltpu.VMEM((MOE_TILE, f), BF16),
                            pltpu.SemaphoreType.DMA((2,)), pltpu.SemaphoreType.DMA((2,)),
                            pltpu.SemaphoreType.DMA(())]),
        compiler_params=_cparams(("arbitrary",)),
        name="moe_perm_ffn",
    )(tile_e, inv, h, g, weg, weu, wed)


def _merge_kernel(h_ref, meta_ref, y0_ref, y1_ref, pp_ref, ps_ref, gp_ref, wpg_ref, wple_ref, gf_ref,
                  *outs, n_p, final):
    i = pl.program_id(0)
    meta = meta_ref[...]
    h = h_ref[...] + (meta[:, 4:5] * y0_ref[...] + meta[:, 5:6] * y1_ref[...])
    p = jnp.where(i < n_p, pp_ref[...], ps_ref[...])
    h = _ple(h, p, gp_ref[...], wpg_ref[...], wple_ref[...])
    if final:
        h = _rms(h, gf_ref[...])
        op_ref, os_ref = outs

        @pl.when(i < n_p)
        def _():
            op_ref[...] = h

        @pl.when(i >= n_p)
        def _():
            os_ref[...] = h
    else:
        outs[0][...] = h


def _merge_ple(h, meta, ys, pp, ps, layer, gp, wpg, wple, gf, tm, n_p, final):
    t, d = h.shape
    ple = pp.shape[-1]
    row = lambda off: pl.BlockSpec((tm, d), lambda i: (off + i, 0))
    if final:
        out_shape = (jax.ShapeDtypeStruct((n_p * tm, d), F32), jax.ShapeDtypeStruct((t - n_p * tm, d), F32))
        out_specs = (pl.BlockSpec((tm, d), lambda i: (jnp.minimum(i, n_p - 1), 0)),
                     pl.BlockSpec((tm, d), lambda i: (jnp.maximum(i - n_p, 0), 0)))
    else:
        out_shape = jax.ShapeDtypeStruct((t, d), F32)
        out_specs = row(0)
    return pl.pallas_call(
        functools.partial(_merge_kernel, n_p=n_p, final=final),
        out_shape=out_shape,
        grid=(t // tm,),
        in_specs=[row(0), pl.BlockSpec((tm, LANES), lambda i: (i, 0)), row(0), row(t // tm)]
                 + _p_specs(layer, tm, n_p, ple)
                 + [_const_spec((1, d)), _const_spec((d, d)), _const_spec((ple, d)), _const_spec((1, d))],
        out_specs=out_specs,
        compiler_params=_cparams(("arbitrary",)),
        name="moe_merge",
    )(h, meta, ys, ys, pp, ps, gp, wpg, wple, gf)


def _latent_kernel(h_ref, g_ref, w_ref, gc_ref, cos_ref, sin_ref, wk_ref, ek_ref, wv_ref, one_ref,
                   c_ref, kpe_ref, k_ref, v_ref):
    r = c_ref.shape[1]
    xn = _rms(h_ref[...], g_ref[...]).astype(BF16)
    z = _dot(xn, w_ref[...])
    c = _rms(z[:, :r], gc_ref[...])
    kpe = z[:, r:r + LANES] * cos_ref[...] + z[:, r + LANES:r + 2 * LANES] * sin_ref[...]
    c_ref[...] = c
    kpe_ref[...] = kpe
    cb = c.astype(BF16)
    k_ref[...] = (_dot(cb, wk_ref[...]) + _dot(kpe.astype(BF16), ek_ref[...])).astype(BF16)
    v_ref[...] = (_dot(cb, wv_ref[...]) + one_ref[...]).astype(BF16)


def _tbl_spec(tm, n_p, n_sblk):
    return pl.BlockSpec((tm, LANES), lambda i: (jnp.where(i < n_p, i % n_sblk, n_sblk + i - n_p), 0))


def _latent(h, g, w, gc, cos_t, sin_t, wk, ek, wv, ones, tm, n_p, n_sblk):
    t, d = h.shape
    r = gc.shape[1]
    hk = wk.shape[1]
    row = lambda w_: pl.BlockSpec((tm, w_), lambda i: (i, 0))
    return pl.pallas_call(
        _latent_kernel,
        out_shape=(jax.ShapeDtypeStruct((t, r), F32), jax.ShapeDtypeStruct((t, LANES), F32),
                   jax.ShapeDtypeStruct((t, hk), BF16), jax.ShapeDtypeStruct((t, hk), BF16)),
        grid=(t // tm,),
        in_specs=[row(d), _const_spec((1, d)), _const_spec(w.shape), _const_spec((1, r)),
                  _tbl_spec(tm, n_p, n_sblk), _tbl_spec(tm, n_p, n_sblk),
                  _const_spec(wk.shape), _const_spec(ek.shape), _const_spec(wv.shape), _const_spec(ones.shape)],
        out_specs=(row(r), row(LANES), row(hk), row(hk)),
        compiler_params=_cparams(("parallel",)),
        name="mla_latent",
    )(h, g, w, gc, cos_t, sin_t, wk, ek, wv, ones)


def _query_kernel(h_ref, g_ref, wdq_ref, gq_ref, wq_ref, wqr_ref, cos_ref, sin_ref, q_ref, *, scale):
    xn = _rms(h_ref[...], g_ref[...]).astype(BF16)
    cq = _rms(_dot(xn, wdq_ref[...]), gq_ref[...]).astype(BF16)
    q = _dot(cq, wq_ref[...])
    qr = _dot(cq, wqr_ref[...])
    cos = cos_ref[...] * scale
    sin = sin_ref[...] * scale
    for hd in range(q.shape[1] // LANES):
        lanes = slice(hd * LANES, (hd + 1) * LANES)
        q_ref[:, lanes] = (q[:, lanes] * cos + qr[:, lanes] * sin).astype(BF16)


def _queries(h, g, wdq, gq, wq, wqr, cos_t, sin_t, scale, tm, n_p, n_sblk):
    t, d = h.shape
    hk = wq.shape[1]
    return pl.pallas_call(
        functools.partial(_query_kernel, scale=scale),
        out_shape=jax.ShapeDtypeStruct((t, hk), BF16),
        grid=(t // tm,),
        in_specs=[pl.BlockSpec((tm, d), lambda i: (i, 0)), _const_spec((1, d)), _const_spec(wdq.shape),
                  _const_spec(gq.shape), _const_spec(wq.shape), _const_spec(wqr.shape),
                  _tbl_spec(tm, n_p, n_sblk), _tbl_spec(tm, n_p, n_sblk)],
        out_specs=pl.BlockSpec((tm, hk), lambda i: (i, 0)),
        compiler_params=_cparams(("parallel",)),
        name="mla_queries",
    )(h, g, wdq, gq, wq, wqr, cos_t, sin_t)


def _attn_kernel(q_ref, k_ref, v_ref, o_ref, s00, s01, s10, s11, acc0, acc1, m0, m1, *, tq, tk):
    qi = pl.program_id(2)
    half = LANES // 2
    s_refs = ((s00, s01), (s10, s11))
    acc_refs = (acc0, acc1)
    m_refs = (m0, m1)
    lanes = (slice(0, LANES), slice(LANES, 2 * LANES))
    shift = CHUNK.bit_length() - 1

    def produce(hd, slot, j):
        start = pl.multiple_of(j * tk, tk)
        s_refs[hd][slot][...] = _dot_nt(q_ref[:, lanes[hd]], k_ref[pl.ds(start, tk), lanes[hd]])

    def consume(hd, slot, j, masked):
        start = pl.multiple_of(j * tk, tk)
        s = s_refs[hd][slot][...]
        if masked:
            qpos = qi * tq + lax.broadcasted_iota(jnp.int32, (tq, tk), 0)
            kpos = j * tk + lax.broadcasted_iota(jnp.int32, (tq, tk), 1)
            s = jnp.where(lax.shift_right_logical(kpos, shift) <= lax.shift_right_logical(qpos, shift), s, NEG)
        m_old = m_refs[hd][...]
        m_new = jnp.maximum(m_old, jnp.broadcast_to(jnp.max(s, axis=-1, keepdims=True), m_old.shape))
        p = jnp.exp2(s - jnp.concatenate([m_new] * (tk // LANES), axis=1)).astype(BF16)
        acc_refs[hd][...] = (jnp.exp2(m_old - m_new) * acc_refs[hd][...]
                             + _dot(p, v_ref[pl.ds(start, tk), lanes[hd]]))
        m_refs[hd][...] = m_new

    for hd in range(2):
        m_refs[hd][...] = jnp.full(m_refs[hd].shape, NEG, F32)
        acc_refs[hd][...] = jnp.zeros(acc_refs[hd].shape, F32)
        produce(hd, 0, 0)

    assert tq == tk

    def pair(jj, carry):
        a = 2 * jj
        for hd in range(2):
            produce(hd, 1, a + 1)
        for hd in range(2):
            consume(hd, 0, a, False)
        for hd in range(2):
            produce(hd, 0, a + 2)
        for hd in range(2):
            consume(hd, 1, a + 1, False)
        return carry

    lax.fori_loop(0, qi // 2, pair, 0)
    a = 2 * (qi // 2)

    @pl.when(qi % 2 == 0)
    def _():
        for hd in range(2):
            consume(hd, 0, a, True)

    @pl.when(qi % 2 == 1)
    def _():
        for hd in range(2):
            produce(hd, 1, a + 1)
        for hd in range(2):
            consume(hd, 0, a, False)
        for hd in range(2):
            consume(hd, 1, a + 1, True)

    acc_e, acc_o = acc0[...], acc1[...]
    o_e = acc_e / acc_e[:, half:half + 1]
    o_o = acc_o / acc_o[:, 0:1]
    lane = lax.broadcasted_iota(jnp.int32, (tq, LANES), 1)
    o_ref[...] = jnp.where(lane < half, o_e, o_o).astype(BF16)


def _prompt_attention(q, k, v, n_b, seq, n_heads, tq, tk):
    nq = seq // tq
    hv = n_heads * (LANES // 2)
    return pl.pallas_call(
        functools.partial(_attn_kernel, tq=tq, tk=tk),
        out_shape=jax.ShapeDtypeStruct((n_b * seq, hv), BF16),
        grid=(n_b, n_heads // 2, nq),
        in_specs=[pl.BlockSpec((tq, 2 * LANES), lambda b, hp, i: (b * nq + i, hp)),
                  pl.BlockSpec((seq, 2 * LANES), lambda b, hp, i: (b, hp)),
                  pl.BlockSpec((seq, 2 * LANES), lambda b, hp, i: (b, hp))],
        out_specs=pl.BlockSpec((tq, LANES), lambda b, hp, i: (b * nq + i, hp)),
        scratch_shapes=[pltpu.VMEM((tq, tk), F32)] * 4 + [pltpu.VMEM((tq, LANES), F32)] * 2
                       + [pltpu.VMEM((tq, LANES), F32)] * 2,
        compiler_params=_cparams(("parallel", "parallel", "arbitrary")),
        name="prompt_attention",
    )(q, k, v)


def _qfeat_kernel(q_ref, w_ref, o_ref):
    o_ref[...] = _dot(q_ref[...], w_ref[...]).astype(BF16)


def _sample_qfeat(q, wq2f, row0_blk, ts):
    n_heads, _, fw = wq2f.shape
    return pl.pallas_call(
        _qfeat_kernel,
        out_shape=jax.ShapeDtypeStruct((n_heads, ts, fw), BF16),
        grid=(n_heads,),
        in_specs=[pl.BlockSpec((ts, LANES), lambda hd: (row0_blk, hd)),
                  pl.BlockSpec((None, LANES, fw), lambda hd: (hd, 0, 0))],
        out_specs=pl.BlockSpec((None, ts, fw), lambda hd: (hd, 0, 0)),
        compiler_params=_cparams(("parallel",)),
        name="sample_qfeat",
    )(q, wq2f)


def _sample_attn_kernel(qf_ref, cc_ref, ckpe_ref, cn_ref, kn_ref, e_ref, o_ref):
    n_heads, sq, fw = qf_ref.shape
    r = cc_ref.shape[1]
    qf = qf_ref[...].reshape(n_heads * sq, fw)
    qc, qp = qf[:, :r], qf[:, r:]
    kc = cc_ref[...].astype(BF16)
    kp = _dot(ckpe_ref[...].astype(BF16), e_ref[...]).astype(BF16)
    kcn = cn_ref[...].astype(BF16)
    kpn = kn_ref[...].astype(BF16)
    s1 = _dot_nt(qc, kc) + _dot_nt(qp, kp)
    s2 = _dot_nt(qc, kcn) + _dot_nt(qp, kpn)
    m = jnp.maximum(jnp.max(s1, axis=-1, keepdims=True), jnp.max(s2, axis=-1, keepdims=True))
    p1 = jnp.exp2(s1 - m)
    p2 = jnp.exp2(s2 - m)
    l = jnp.sum(p1, axis=-1, keepdims=True) + jnp.sum(p2, axis=-1, keepdims=True)
    o = (_dot(p1.astype(BF16), kc) + _dot(p2.astype(BF16), kcn)) / l
    o_ref[...] = o.astype(BF16).reshape(n_heads, sq, r)


def _sample_attention(qf, cache_c, cache_kpe, c_all, kpe_all, e_place, tp, sq):
    n_heads, ts, fw = qf.shape
    n_bs, past, r = cache_c.shape
    rope = cache_kpe.shape[-1]
    return pl.pallas_call(
        _sample_attn_kernel,
        out_shape=jax.ShapeDtypeStruct((n_heads, ts, r), BF16),
        grid=(n_bs,),
        in_specs=[pl.BlockSpec((n_heads, sq, fw), lambda b: (0, b, 0)),
                  pl.BlockSpec((None, past, r), lambda b: (b, 0, 0)),
                  pl.BlockSpec((None, past, rope), lambda b: (b, 0, 0)),
                  pl.BlockSpec((sq, r), lambda b: (tp // sq + b, 0)),
                  pl.BlockSpec((sq, LANES), lambda b: (tp // sq + b, 0)),
                  _const_spec(e_place.shape)],
        out_specs=pl.BlockSpec((n_heads, sq, r), lambda b: (0, b, 0)),
        compiler_params=_cparams(("parallel",)),
        name="sample_attention",
    )(qf, cache_c, cache_kpe, c_all, kpe_all, e_place)


def _sample_uv_kernel(ol_ref, wv_ref, o_ref):
    o_ref[...] = (_dot(ol_ref[0], wv_ref[0]) + _dot(ol_ref[1], wv_ref[1])).astype(BF16)


def _sample_uv(o_lat, wv_pair):
    n_heads, ts, r = o_lat.shape
    return pl.pallas_call(
        _sample_uv_kernel,
        out_shape=jax.ShapeDtypeStruct((ts, n_heads * (LANES // 2)), BF16),
        grid=(n_heads // 2,),
        in_specs=[pl.BlockSpec((2, ts, r), lambda hp: (hp, 0, 0)),
                  pl.BlockSpec((2, r, LANES), lambda hp: (hp, 0, 0))],
        out_specs=pl.BlockSpec((ts, LANES), lambda hp: (0, hp)),
        compiler_params=_cparams(("parallel",)),
        name="sample_uv",
    )(o_lat, wv_pair)


def _oproj_kernel(op_ref, os_ref, w_ref, h_ref, out_ref, *, n_p):
    o = jnp.where(pl.program_id(0) < n_p, op_ref[...], os_ref[...])
    out_ref[...] = h_ref[...] + _dot(o, w_ref[...])


def _oproj(o_p, o_s, w, h, tm, n_p):
    t, d = h.shape
    hv = o_p.shape[1]
    return pl.pallas_call(
        functools.partial(_oproj_kernel, n_p=n_p),
        out_shape=jax.ShapeDtypeStruct((t, d), F32),
        grid=(t // tm,),
        in_specs=[pl.BlockSpec((tm, hv), lambda i: (jnp.minimum(i, n_p - 1), 0)),
                  pl.BlockSpec((tm, hv), lambda i: (jnp.maximum(i - n_p, 0), 0)),
                  _const_spec(w.shape),
                  pl.BlockSpec((tm, d), lambda i: (i, 0))],
        out_specs=pl.BlockSpec((tm, d), lambda i: (i, 0)),
        compiler_params=_cparams(("parallel",)),
        name="attn_oproj",
    )(o_p, o_s, w, h)


def _rot_cols(w):
    half = w.shape[-1] // 2
    return jnp.concatenate([-w[..., half:], w[..., :half]], axis=-1)


def _moe_schedule(meta, cnt, n_exp, nt_max):
    counts = cnt[0, :n_exp].astype(jnp.int32)
    ntile = (counts + MOE_TILE - 1) // MOE_TILE
    tile_end = jnp.cumsum(ntile)
    seg_start = (tile_end - ntile) * MOE_TILE
    n_tiles = tile_end[-1]
    e = meta[:, 0:TOP_K].astype(jnp.int32)
    rank = meta[:, TOP_K:2 * TOP_K].astype(jnp.int32)
    pos = (seg_start[e] + rank).reshape(-1)
    tid = jnp.minimum(jnp.arange(nt_max, dtype=jnp.int32), n_tiles - 1)
    tile_e = jnp.sum((tid[:, None] >= tile_end[None, :]).astype(jnp.int32), axis=1)
    lo = jnp.concatenate([seg_start + counts, (n_tiles * MOE_TILE).reshape(1)]).astype(jnp.int32)
    hi = jnp.concatenate([tile_end * MOE_TILE, jnp.full((1,), nt_max * MOE_TILE)]).astype(jnp.int32)
    return pos.astype(jnp.int32), tile_e.astype(jnp.int32), lo, hi


def kernel(x_prompt, x_sample, state_conv, cache_ckv, cache_kpe, p_prompt, p_sample, g_mix, g_ffn, w_pw1, b_pw1, w_dw, b_dw, ln_g, ln_b, w_pw2, b_pw2, g_kv, w_dkv, g_ckv, w_uk, w_uv, w_dq, g_q, w_uq, w_o, w_gate, w_up, w_down, w_router, we_gate, we_up, we_down, w_ple, w_ple_gate, g_ple, g_final):
    n_b, seq, d = x_prompt.shape
    n_bs, seq_s, _ = x_sample.shape
    depth = g_mix.shape[0]
    n_a = w_pw1.shape[0]
    taps = w_dw.shape[1]
    past = cache_ckv.shape[1]
    r_kv = cache_ckv.shape[2]
    rope = cache_kpe.shape[2]
    n_heads, nope = w_uk.shape[1], w_uk.shape[2]
    v_dim = w_uv.shape[2]
    n_exp = w_router.shape[2]
    ple = p_prompt.shape[-1]
    tp, ts = n_b * seq, n_bs * seq_s
    t = tp + ts
    tm = ts
    assert tp % tm == 0 and seq % tm == 0 and tm % MOE_TILE == 0
    assert nope == LANES // 2 and v_dim == LANES // 2 and nope + rope <= LANES and taps - 1 <= HALO
    n_p = tp // tm
    n_sblk = seq // tm
    scale = float(nope + rope) ** -0.5 * math.log2(math.e)
    row = lambda a: a.reshape(1, -1)
    bf = lambda a: a.astype(BF16)

    h = jnp.concatenate([x_prompt.reshape(tp, d), x_sample.reshape(ts, d)], axis=0)
    pp = p_prompt.reshape(depth, tp, ple)
    ps = p_sample.reshape(depth, ts, ple)

    posn = jnp.concatenate([jnp.arange(seq), past + (jnp.arange(ts) % seq_s)]).astype(F32)
    inv = ROPE_THETA ** (-jnp.arange(0, rope, 2, dtype=F32) / rope)
    ang = posn[:, None] * inv[None, :]
    ang = jnp.concatenate([ang, ang], axis=-1)
    pad_r = LANES - nope - rope
    cos_t = jnp.concatenate([jnp.ones((posn.shape[0], nope), F32), jnp.cos(ang),
                             jnp.zeros((posn.shape[0], pad_r), F32)], axis=1)
    sin_t = jnp.concatenate([jnp.zeros((posn.shape[0], nope), F32), jnp.sin(ang),
                             jnp.zeros((posn.shape[0], pad_r), F32)], axis=1)

    conv_p, conv_s = [], []
    c_all = kpe_all = k_cat = v_cat = None
    nt_max = (TOP_K * t) // MOE_TILE + n_exp
    rows_pad = nt_max * MOE_TILE

    for i in range(depth):
        if i < n_a:
            a = i
            u = _pw1_glu(h, row(g_mix[i]), bf(w_pw1[a]), row(b_pw1[a]), tm)
            st_pad = jnp.pad(state_conv[a], ((0, 0), (HALO - (taps - 1), 0), (0, 0))).reshape(n_bs * HALO, d)
            h = _conv_block(u, h, st_pad, w_dw[a], row(b_dw[a]), row(ln_g[a]), row(ln_b[a]),
                            bf(w_pw2[a]), row(b_pw2[a]), n_b=n_b, seq=seq, n_bs=n_bs, seq_s=seq_s, tt=tm)
            conv_p.append(jnp.stack([u[(bi + 1) * seq - (taps - 1):(bi + 1) * seq] for bi in range(n_b)]))
            us = u[tp:].reshape(n_bs, seq_s, d)
            conv_s.append(jnp.concatenate([state_conv[a], us], axis=1)[:, -(taps - 1):])
        else:
            b = i - n_a
            wq = w_uq[b]
            zq = jnp.zeros(wq.shape[:2] + (pad_r,), F32)
            wq_cat = jnp.concatenate([wq, zq], axis=-1).reshape(wq.shape[0], n_heads * LANES)
            wq_rot = jnp.concatenate([jnp.zeros(wq.shape[:2] + (nope,), F32), _rot_cols(wq[..., nope:]), zq],
                                     axis=-1).reshape(wq.shape[0], n_heads * LANES)
            q = _queries(h, row(g_mix[i]), bf(w_dq[b]), row(g_q[b]), bf(wq_cat), bf(wq_rot),
                         cos_t, sin_t, scale, tm, n_p, n_sblk)
            o_p = _prompt_attention(q, k_cat, v_cat, n_b, seq, n_heads, tq=tm, tk=tm)
            wk_t = jnp.transpose(w_uk, (1, 2, 0))
            sel = jnp.zeros((LANES, LANES), F32).at[jnp.arange(nope, nope + rope), jnp.arange(nope, nope + rope)].set(1.0)
            wq2f = jnp.concatenate([
                jnp.concatenate([wk_t, jnp.zeros((n_heads, LANES - nope, r_kv), F32)], axis=1),
                jnp.broadcast_to(sel, (n_heads, LANES, LANES))], axis=2)
            qf = _sample_qfeat(q, bf(wq2f), n_p, ts)
            e_place = jnp.zeros((rope, LANES), F32).at[jnp.arange(rope), nope + jnp.arange(rope)].set(1.0)
            o_lat = _sample_attention(qf, cache_ckv, cache_kpe, c_all, kpe_all, bf(e_place), tp, seq_s)
            wv_h = jnp.transpose(w_uv, (1, 0, 2))
            zv = jnp.zeros_like(wv_h)
            wv_pair = jnp.where((jnp.arange(n_heads) % 2 == 0)[:, None, None],
                                jnp.concatenate([wv_h, zv], axis=2), jnp.concatenate([zv, wv_h], axis=2))
            o_s = _sample_uv(o_lat, bf(wv_pair))
            h = _oproj(o_p, o_s, bf(w_o[b].reshape(n_heads * v_dim, d)), h, tm, n_p)

        j = i // 2
        last = i == depth - 1
        if i % 2 == 0:
            h = _ffn_ple(h, row(g_ffn[i]), bf(w_gate[j]), bf(w_up[j]), bf(w_down[j]), pp, ps, i,
                         row(g_ple[i]), bf(w_ple_gate[i]), bf(w_ple[i]), tm, n_p)
        else:
            wr = jnp.pad(w_router[j], ((0, 0), (0, LANES - n_exp)))
            wr_hi = bf(wr)
            wr_lo = bf(wr - wr_hi.astype(F32))
            meta, cnt = _router(h, row(g_ffn[i]), wr_hi, wr_lo, n_exp, tm)
            pos, tile_e, pad_lo, pad_hi = _moe_schedule(meta, cnt, n_exp, nt_max)
            inv = _moe_inverse(pos, pad_lo, pad_hi, rows_pad)
            ys = _moe_perm_ffn(tile_e, inv, h, row(g_ffn[i]), bf(we_gate), bf(we_up), bf(we_down), j)
            h = _merge_ple(h, meta, ys, pp, ps, i, row(g_ple[i]), bf(w_ple_gate[i]), bf(w_ple[i]),
                           row(g_final), tm, n_p, final=last)
        if last and i % 2 == 0:
            raise NotImplementedError("final norm is fused into the MoE combine of the last layer")

        if i == n_a - 1:
            w_ext = jnp.concatenate([
                w_dkv[:, :r_kv],
                jnp.zeros((d, nope), F32), w_dkv[:, r_kv:], jnp.zeros((d, pad_r), F32),
                jnp.zeros((d, nope), F32), _rot_cols(w_dkv[:, r_kv:]), jnp.zeros((d, pad_r), F32)], axis=1)
            wk_cat = jnp.concatenate([w_uk, jnp.zeros((r_kv, n_heads, LANES - nope), F32)], axis=2)
            wk_cat = wk_cat.reshape(r_kv, n_heads * LANES)
            lane = jnp.arange(LANES)
            e_k = ((lane[:, None] == (jnp.arange(n_heads * LANES) % LANES)[None, :])
                   & (lane[:, None] >= nope) & (lane[:, None] < nope + rope)).astype(F32)
            zv = jnp.zeros_like(w_uv)
            wv_cat = jnp.where((jnp.arange(n_heads) % 2 == 0)[None, :, None],
                               jnp.concatenate([w_uv, zv], axis=2), jnp.concatenate([zv, w_uv], axis=2))
            wv_cat = wv_cat.reshape(r_kv, n_heads * LANES)
            col = jnp.arange(n_heads * LANES)
            ones_lane = jnp.where((col // LANES) % 2 == 0, v_dim, 0)
            v_ones = (col % LANES == ones_lane).astype(F32).reshape(1, -1)
            c_all, kpe_all, k_cat, v_cat = _latent(h, row(g_kv), bf(w_ext), row(g_ckv), cos_t, sin_t,
                                                   bf(wk_cat), bf(e_k), bf(wv_cat), v_ones, tm, n_p, n_sblk)

    kpe_out = kpe_all[:, nope:nope + rope]
    y_p, y_s = h
    return (y_p.reshape(n_b, seq, d), y_s.reshape(n_bs, seq_s, d),
            jnp.stack(conv_p), jnp.stack(conv_s),
            c_all[:tp].reshape(n_b, seq, r_kv), kpe_out[:tp].reshape(n_b, seq, rope),
            c_all[tp:].reshape(n_bs, seq_s, r_kv), kpe_out[tp:].reshape(n_bs, seq_s, rope))
```

```python
import functools
import math

import jax
import jax.numpy as jnp
from jax import lax
from jax.experimental import pallas as pl
from jax.experimental.pallas import tpu as pltpu

EPS = 1e-6
NEG = -1e30
CHUNK = 64
ROPE_THETA = 10000.0
TOP_K = 2

LANES = 128
HALO = 32
MOE_TILE = 256
ISSUE_UNROLL = 8
VMEM_LIMIT = 60 * 1024 * 1024

F32 = jnp.float32
BF16 = jnp.bfloat16


def _cparams(sem=None):
    return pltpu.CompilerParams(dimension_semantics=sem, vmem_limit_bytes=VMEM_LIMIT)


def _const_spec(shape):
    nd = len(shape)
    return pl.BlockSpec(shape, lambda *_: (0,) * nd, pipeline_mode=pl.Buffered(1))


def _rms(x, g):
    return x * lax.rsqrt(jnp.mean(x * x, axis=-1, keepdims=True) + EPS) * g


def _dot(a, b):
    return jnp.dot(a, b, preferred_element_type=F32)


def _dot_nt(a, b):
    return lax.dot_general(a, b, (((1,), (1,)), ((), ())), preferred_element_type=F32)


def _sigmoid(x):
    return 1.0 / (1.0 + jnp.exp(-x))


def _silu(x):
    return x * _sigmoid(x)


def _ff_chunk(f, cap=1536):
    best = None
    for c in range(LANES, min(f, cap) + 1, LANES):
        if f % c == 0:
            best = c
    return best if best is not None else f


def _pw1_kernel(h_ref, g_ref, w_ref, b_ref, u_ref):
    d = h_ref.shape[1]
    xn = _rms(h_ref[...], g_ref[...]).astype(BF16)
    a = _dot(xn, w_ref[...]) + b_ref[...]
    u_ref[...] = a[:, :d] * _sigmoid(a[:, d:])


def _pw1_glu(h, g, w, b, tm):
    t, d = h.shape
    return pl.pallas_call(
        _pw1_kernel,
        out_shape=jax.ShapeDtypeStruct((t, d), F32),
        grid=(t // tm,),
        in_specs=[pl.BlockSpec((tm, d), lambda i: (i, 0)),
                  _const_spec((1, d)), _const_spec((d, 2 * d)), _const_spec((1, 2 * d))],
        out_specs=pl.BlockSpec((tm, d), lambda i: (i, 0)),
        compiler_params=_cparams(("parallel",)),
        name="pw1_glu",
    )(h, g, w, b)


def _conv_kernel(cur_ref, prev_ref, h_ref, wdw_ref, bdw_ref, lg_ref, lb_ref, w2_ref, b2_ref,
                 o_ref, full_ref, sh_ref, y_ref, *, taps, zero_first):
    tt, d = cur_ref.shape
    nch = d // LANES
    prev = prev_ref[...]
    if zero_first:
        prev = jnp.where(pl.program_id(1) == 0, 0.0, prev)
    for c in range(nch):
        full_ref[c, 0:HALO, :] = prev[:, c * LANES:(c + 1) * LANES]
        full_ref[c, HALO:HALO + tt, :] = cur_ref[:, c * LANES:(c + 1) * LANES]
    off = HALO - (taps - 1)
    n_sh = sh_ref.shape[1]
    rc = min(tt, 32)

    def chunk(c, carry):
        for s in range(1, 8):
            sh_ref[s - 1] = full_ref[c, pl.ds(s, n_sh), :]
        wk = [wdw_ref[c, k:k + 1, :] for k in range(taps)]
        for r in range(tt // rc):
            acc = None
            for k in range(taps):
                j, s = divmod(k + off, 8)
                rows = pl.ds(r * rc + 8 * j, rc)
                term = wk[k] * (full_ref[c, rows, :] if s == 0 else sh_ref[s - 1, rows, :])
                acc = term if acc is None else acc + term
            y_ref[c, r * rc:(r + 1) * rc, :] = acc
        return carry

    lax.fori_loop(0, nch, chunk, 0)
    y = jnp.concatenate([y_ref[c] for c in range(nch)], axis=1) + bdw_ref[...]
    mu = jnp.mean(y, axis=-1, keepdims=True)
    yc = y - mu
    var = jnp.mean(yc * yc, axis=-1, keepdims=True)
    z = _silu(yc * lax.rsqrt(var + EPS) * lg_ref[...] + lb_ref[...])
    o_ref[...] = h_ref[...] + _dot(z.astype(BF16), w2_ref[...]) + b2_ref[...]


def _conv_block(u, h, st_pad, wdw, bdw, lg, lb, w2, b2, *, n_b, seq, n_bs, seq_s, tt):
    t, d = h.shape
    tp = n_b * seq
    taps = wdw.shape[0]
    nblk = seq // tt
    nch = d // LANES
    wdw = jnp.transpose(wdw.reshape(taps, nch, LANES), (1, 0, 2))
    consts = [_const_spec(wdw.shape), _const_spec((1, d)), _const_spec((1, d)), _const_spec((1, d)),
              _const_spec((d, d)), _const_spec((1, d))]
    scratch = lambda rows: [pltpu.VMEM((nch, HALO + rows, LANES), F32),
                            pltpu.VMEM((7, HALO + rows - 8, LANES), F32),
                            pltpu.VMEM((nch, rows, LANES), F32)]
    cur_p = lambda b, i: (b * nblk + i, 0)
    h = pl.pallas_call(
        functools.partial(_conv_kernel, taps=taps, zero_first=True),
        out_shape=jax.ShapeDtypeStruct((t, d), F32),
        grid=(n_b, nblk),
        in_specs=[pl.BlockSpec((tt, d), cur_p),
                  pl.BlockSpec((HALO, d), lambda b, i: (jnp.maximum((b * seq + i * tt) // HALO - 1, 0), 0)),
                  pl.BlockSpec((tt, d), cur_p)] + consts,
        out_specs=pl.BlockSpec((tt, d), cur_p),
        scratch_shapes=scratch(tt),
        input_output_aliases={2: 0},
        compiler_params=_cparams(("parallel", "parallel")),
        name="conv_prompt",
    )(u, u, h, wdw, bdw, lg, lb, w2, b2)
    cur_s = lambda b: (tp // seq_s + b, 0)
    return pl.pallas_call(
        functools.partial(_conv_kernel, taps=taps, zero_first=False),
        out_shape=jax.ShapeDtypeStruct((t, d), F32),
        grid=(n_bs,),
        in_specs=[pl.BlockSpec((seq_s, d), cur_s),
                  pl.BlockSpec((HALO, d), lambda b: (b, 0)),
                  pl.BlockSpec((seq_s, d), cur_s)] + consts,
        out_specs=pl.BlockSpec((seq_s, d), cur_s),
        scratch_shapes=scratch(seq_s),
        input_output_aliases={2: 0},
        compiler_params=_cparams(("parallel",)),
        name="conv_sample",
    )(u, st_pad, h, wdw, bdw, lg, lb, w2, b2)


def _ple(h, p, gp, wpg, wple):
    gate = _sigmoid(_dot(_rms(h, gp).astype(BF16), wpg))
    return h + _dot(p.astype(BF16), wple) * gate


def _p_specs(layer, tm, n_p, ple):
    return [pl.BlockSpec((None, tm, ple), lambda i, *_: (layer, jnp.minimum(i, n_p - 1), 0)),
            pl.BlockSpec((None, tm, ple), lambda i, *_: (layer, jnp.maximum(i - n_p, 0), 0))]


def _ffn_kernel(h_ref, g_ref, wg_ref, wu_ref, wd_ref, pp_ref, ps_ref, gp_ref, wpg_ref, wple_ref,
                o_ref, act_ref, *, n_p, fc):
    h = h_ref[...]
    xn = _rms(h, g_ref[...]).astype(BF16)
    f = wg_ref.shape[1]
    for c in range(f // fc):
        cols = slice(c * fc, (c + 1) * fc)
        act_ref[:, cols] = (_silu(_dot(xn, wg_ref[:, cols])) * _dot(xn, wu_ref[:, cols])).astype(BF16)
    h = h + _dot(act_ref[...], wd_ref[...])
    p = jnp.where(pl.program_id(0) < n_p, pp_ref[...], ps_ref[...])
    o_ref[...] = _ple(h, p, gp_ref[...], wpg_ref[...], wple_ref[...])


def _ffn_ple(h, g, wg, wu, wd, pp, ps, layer, gp, wpg, wple, tm, n_p):
    t, d = h.shape
    f = wg.shape[1]
    ple = pp.shape[-1]
    return pl.pallas_call(
        functools.partial(_ffn_kernel, n_p=n_p, fc=_ff_chunk(f, cap=MOE_TILE)),
        out_shape=jax.ShapeDtypeStruct((t, d), F32),
        grid=(t // tm,),
        in_specs=[pl.BlockSpec((tm, d), lambda i: (i, 0)), _const_spec((1, d)),
                  _const_spec((d, f)), _const_spec((d, f)), _const_spec((f, d))]
                 + _p_specs(layer, tm, n_p, ple)
                 + [_const_spec((1, d)), _const_spec((d, d)), _const_spec((ple, d))],
        out_specs=pl.BlockSpec((tm, d), lambda i: (i, 0)),
        scratch_shapes=[pltpu.VMEM((tm, f), BF16)],
        compiler_params=_cparams(("parallel",)),
        name="ffn_ple",
    )(h, g, wg, wu, wd, pp, ps, gp, wpg, wple)


def _router_kernel(h_ref, g_ref, whi_ref, wlo_ref, meta_ref, cnt_ref, run_ref, *, n_exp):
    i = pl.program_id(0)

    @pl.when(i == 0)
    def _():
        run_ref[...] = jnp.zeros_like(run_ref)

    tm = h_ref.shape[0]
    xn = _rms(h_ref[...], g_ref[...])
    hi = xn.astype(BF16)
    lo = (xn - hi.astype(F32)).astype(BF16)
    logits = _dot(hi, whi_ref[...]) + (_dot(hi, wlo_ref[...]) + _dot(lo, whi_ref[...]))
    lane = lax.broadcasted_iota(jnp.int32, (tm, LANES), 1).astype(F32)
    logits = jnp.where(lane < n_exp, logits, -jnp.inf)
    m1 = jnp.max(logits, axis=-1, keepdims=True)
    i1 = jnp.min(jnp.where(logits == m1, lane, float(LANES)), axis=-1, keepdims=True)
    rest = jnp.where(lane == i1, -jnp.inf, logits)
    m2 = jnp.max(rest, axis=-1, keepdims=True)
    i2 = jnp.min(jnp.where(rest == m2, lane, float(LANES)), axis=-1, keepdims=True)
    e2 = jnp.exp(m2 - m1)
    w1 = 1.0 / (1.0 + e2)
    w2 = e2 / (1.0 + e2)
    oh1 = (lane == i1).astype(F32)
    oh2 = (lane == i2).astype(F32)
    oh = oh1 + oh2
    rows = lax.broadcasted_iota(jnp.int32, (tm, tm), 0)
    cols = lax.broadcasted_iota(jnp.int32, (tm, tm), 1)
    tri = jnp.where(cols < rows, 1.0, 0.0).astype(BF16)
    pre = _dot(tri, oh.astype(BF16)) + run_ref[...]
    r1 = jnp.sum(pre * oh1, axis=-1, keepdims=True)
    r2 = jnp.sum(pre * oh2, axis=-1, keepdims=True)
    run = run_ref[...] + jnp.sum(oh, axis=0, keepdims=True)
    run_ref[...] = run
    cnt_ref[...] = jnp.broadcast_to(run, cnt_ref.shape)
    meta = jnp.where(lane == 0, i1, 0.0)
    meta = jnp.where(lane == 1, i2, meta)
    meta = jnp.where(lane == 2, r1, meta)
    meta = jnp.where(lane == 3, r2, meta)
    meta = jnp.where(lane == 4, w1, meta)
    meta = jnp.where(lane == 5, w2, meta)
    meta_ref[...] = meta


def _router(h, g, whi, wlo, n_exp, tm):
    t, d = h.shape
    return pl.pallas_call(
        functools.partial(_router_kernel, n_exp=n_exp),
        out_shape=(jax.ShapeDtypeStruct((t, LANES), F32), jax.ShapeDtypeStruct((8, LANES), F32)),
        grid=(t // tm,),
        in_specs=[pl.BlockSpec((tm, d), lambda i: (i, 0)), _const_spec((1, d)),
                  _const_spec((d, LANES)), _const_spec((d, LANES))],
        out_specs=(pl.BlockSpec((tm, LANES), lambda i: (i, 0)), pl.BlockSpec((8, LANES), lambda i: (0, 0))),
        scratch_shapes=[pltpu.VMEM((1, LANES), F32)],
        compiler_params=_cparams(("arbitrary",)),
        name="router",
    )(h, g, whi, wlo)


def _inverse_kernel(pos_ref, lo_ref, hi_ref, inv_ref, *, n_rng):
    for e in range(n_rng):
        def fill(r, carry):
            inv_ref[r] = -1
            return carry

        lax.fori_loop(lo_ref[e], hi_ref[e], fill, 0)

    def put(idx, carry):
        inv_ref[pos_ref[idx]] = idx
        return carry

    lax.fori_loop(0, pos_ref.shape[0], put, 0, unroll=ISSUE_UNROLL)


def _moe_inverse(pos, lo, hi, rows):
    return pl.pallas_call(
        functools.partial(_inverse_kernel, n_rng=lo.shape[0]),
        out_shape=jax.ShapeDtypeStruct((rows,), jnp.int32),
        grid_spec=pltpu.PrefetchScalarGridSpec(
            num_scalar_prefetch=3, grid=(1,), in_specs=[],
            out_specs=pl.BlockSpec(memory_space=pltpu.SMEM)),
        compiler_params=_cparams(("arbitrary",)),
        name="moe_inverse",
    )(pos, lo, hi)


def _moe_perm_kernel(te_ref, inv_ref, h_ref, g_ref, wg_ref, wu_ref, wd_ref, o_ref,
                     xg_ref, yb_ref, act_ref, gsem, ssem, zsem, *, fc, t_tok, n_steps):
    i = pl.program_id(0)
    last = n_steps - 1
    trash = TOP_K * t_tok

    def gather_row(tile, slot, r):
        idx = inv_ref[tile * MOE_TILE + r]
        tok = lax.shift_right_logical(jnp.maximum(idx, 0), 1)
        pltpu.make_async_copy(h_ref.at[pl.ds(tok, 1)], xg_ref.at[slot, pl.ds(r, 1)], gsem.at[slot]).start()

    def scatter_row(tile, slot, r):
        idx = inv_ref[tile * MOE_TILE + r]
        dest = jnp.where(idx >= 0, (idx & 1) * t_tok + lax.shift_right_logical(idx, 1), trash + r)
        pltpu.make_async_copy(yb_ref.at[slot, pl.ds(r, 1)], o_ref.at[pl.ds(dest, 1)], ssem.at[slot]).start()

    def rolled(row_fn, tile, slot):
        def body(r, carry):
            row_fn(tile, slot, r)
            return carry

        lax.fori_loop(0, MOE_TILE, body, 0, unroll=ISSUE_UNROLL)

    def wait_gather(slot):
        pltpu.make_async_copy(h_ref.at[pl.ds(0, MOE_TILE)], xg_ref.at[slot], gsem.at[slot]).wait()

    def wait_scatter(slot):
        pltpu.make_async_copy(yb_ref.at[slot], o_ref.at[pl.ds(0, MOE_TILE)], ssem.at[slot]).wait()

    def compute(slot, issue_gathers, issue_scatters):
        f = wg_ref.shape[1]
        n_chunk = f // fc
        share = MOE_TILE // n_chunk
        x = _rms(xg_ref[slot], g_ref[...]).astype(BF16)
        issue_gathers()
        for c in range(n_chunk):
            cols = slice(c * fc, (c + 1) * fc)
            gate = _dot(x, wg_ref[:, cols])
            up = _dot(x, wu_ref[:, cols])
            if c == 0:
                row0 = pl.multiple_of(jnp.minimum(te_ref[0], 0) * 8, 8)
                xg_ref[slot, pl.ds(row0, 8), 0:LANES] = up[0:8, 0:LANES]
            act_ref[:, cols] = (_silu(gate) * up).astype(BF16)
            issue_scatters(c * share, MOE_TILE if c == n_chunk - 1 else (c + 1) * share)
        yb_ref[slot] = _dot(act_ref[...], wd_ref[...])

    @pl.when(i == 0)
    def _():
        yb_ref[1] = jnp.zeros(yb_ref.shape[1:], F32)
        cp = pltpu.make_async_copy(yb_ref.at[1], o_ref.at[pl.ds(trash, MOE_TILE)], zsem)
        cp.start()
        cp.wait()
        rolled(gather_row, 0, 0)
        wait_gather(0)
        def issue_gathers():
            for r in range(MOE_TILE):
                gather_row(1, 1, r)

        compute(0, issue_gathers, lambda r0, r1: None)

    for par in range(2):
        @pl.when((i > 0) & (i % 2 == par))
        def _():
            wait_gather(par)

            @pl.when(i >= 2)
            def _():
                wait_scatter(par)

            nxt = jnp.minimum(i + 1, last)

            def issue_gathers():
                for r in range(MOE_TILE):
                    gather_row(nxt, 1 - par, r)

            def issue_scatters(r0, r1):
                for r in range(r0, r1):
                    scatter_row(i - 1, 1 - par, r)

            compute(par, issue_gathers, issue_scatters)

    @pl.when(i == last)
    def _():
        wait_scatter(1 - last % 2)
        rolled(scatter_row, last, last % 2)
        wait_scatter(last % 2)
        wait_gather(1 - last % 2)


def _moe_perm_ffn(tile_e, inv, h, g, weg, weu, wed, layer):
    t, d = h.shape
    f = weg.shape[-1]
    n_steps = tile_e.shape[0]
    wmap = lambda i, te, inv_: (layer, te[i], 0, 0)
    buf = lambda: pltpu.VMEM((2, MOE_TILE, d), F32)
    return pl.pallas_call(
        functools.partial(_moe_perm_kernel, fc=_ff_chunk(f), t_tok=t, n_steps=n_steps),
        out_shape=jax.ShapeDtypeStruct((TOP_K * t + MOE_TILE, d), F32),
        grid_spec=pltpu.PrefetchScalarGridSpec(
            num_scalar_prefetch=2,
            grid=(n_steps,),
            in_specs=[pl.BlockSpec(memory_space=pl.ANY),
                      pl.BlockSpec((1, d), lambda i, te, inv_: (0, 0)),
                      pl.BlockSpec((None, None, d, f), wmap),
                      pl.BlockSpec((None, None, d, f), wmap),
                      pl.BlockSpec((None, None, f, d), wmap)],
            out_specs=pl.BlockSpec(memory_space=pl.ANY),
            scratch_shapes=[buf(), buf(), pltpu.VMEM((MOE_TILE, f), BF16),
                            pltpu.SemaphoreType.DMA((2,)), pltpu.SemaphoreType.DMA((2,)),
                            pltpu.SemaphoreType.DMA(())]),
        compiler_params=_cparams(("arbitrary",)),
        name="moe_perm_ffn",
    )(tile_e, inv, h, g, weg, weu, wed)


def _merge_kernel(h_ref, meta_ref, y0_ref, y1_ref, pp_ref, ps_ref, gp_ref, wpg_ref, wple_ref, gf_ref,
                  *outs, n_p, final):
    i = pl.program_id(0)
    meta = meta_ref[...]
    h = h_ref[...] + (meta[:, 4:5] * y0_ref[...] + meta[:, 5:6] * y1_ref[...])
    p = jnp.where(i < n_p, pp_ref[...], ps_ref[...])
    h = _ple(h, p, gp_ref[...], wpg_ref[...], wple_ref[...])
    if final:
        h = _rms(h, gf_ref[...])
        op_ref, os_ref = outs

        @pl.when(i < n_p)
        def _():
            op_ref[...] = h

        @pl.when(i >= n_p)
        def _():
            os_ref[...] = h
    else:
        outs[0][...] = h


def _merge_ple(h, meta, ys, pp, ps, layer, gp, wpg, wple, gf, tm, n_p, final):
    t, d = h.shape
    ple = pp.shape[-1]
    row = lambda off: pl.BlockSpec((tm, d), lambda i: (off + i, 0))
    if final:
        out_shape = (jax.ShapeDtypeStruct((n_p * tm, d), F32), jax.ShapeDtypeStruct((t - n_p * tm, d), F32))
        out_specs = (pl.BlockSpec((tm, d), lambda i: (jnp.minimum(i, n_p - 1), 0)),
                     pl.BlockSpec((tm, d), lambda i: (jnp.maximum(i - n_p, 0), 0)))
    else:
        out_shape = jax.ShapeDtypeStruct((t, d), F32)
        out_specs = row(0)
    return pl.pallas_call(
        functools.partial(_merge_kernel, n_p=n_p, final=final),
        out_shape=out_shape,
        grid=(t // tm,),
        in_specs=[row(0), pl.BlockSpec((tm, LANES), lambda i: (i, 0)), row(0), row(t // tm)]
                 + _p_specs(layer, tm, n_p, ple)
                 + [_const_spec((1, d)), _const_spec((d, d)), _const_spec((ple, d)), _const_spec((1, d))],
        out_specs=out_specs,
        compiler_params=_cparams(("arbitrary",)),
        name="moe_merge",
    )(h, meta, ys, ys, pp, ps, gp, wpg, wple, gf)


def _latent_kernel(h_ref, g_ref, w_ref, gc_ref, cos_ref, sin_ref, wk_ref, ek_ref, wv_ref, one_ref,
                   c_ref, kpe_ref, k_ref, v_ref):
    r = c_ref.shape[1]
    xn = _rms(h_ref[...], g_ref[...]).astype(BF16)
    z = _dot(xn, w_ref[...])
    c = _rms(z[:, :r], gc_ref[...])
    kpe = z[:, r:r + LANES] * cos_ref[...] + z[:, r + LANES:r + 2 * LANES] * sin_ref[...]
    c_ref[...] = c
    kpe_ref[...] = kpe
    cb = c.astype(BF16)
    k_ref[...] = (_dot(cb, wk_ref[...]) + _dot(kpe.astype(BF16), ek_ref[...])).astype(BF16)
    v_ref[...] = (_dot(cb, wv_ref[...]) + one_ref[...]).astype(BF16)


def _tbl_spec(tm, n_p, n_sblk):
    return pl.BlockSpec((tm, LANES), lambda i: (jnp.where(i < n_p, i % n_sblk, n_sblk + i - n_p), 0))


def _latent(h, g, w, gc, cos_t, sin_t, wk, ek, wv, ones, tm, n_p, n_sblk):
    t, d = h.shape
    r = gc.shape[1]
    hk = wk.shape[1]
    row = lambda w_: pl.BlockSpec((tm, w_), lambda i: (i, 0))
    return pl.pallas_call(
        _latent_kernel,
        out_shape=(jax.ShapeDtypeStruct((t, r), F32), jax.ShapeDtypeStruct((t, LANES), F32),
                   jax.ShapeDtypeStruct((t, hk), BF16), jax.ShapeDtypeStruct((t, hk), BF16)),
        grid=(t // tm,),
        in_specs=[row(d), _const_spec((1, d)), _const_spec(w.shape), _const_spec((1, r)),
                  _tbl_spec(tm, n_p, n_sblk), _tbl_spec(tm, n_p, n_sblk),
                  _const_spec(wk.shape), _const_spec(ek.shape), _const_spec(wv.shape), _const_spec(ones.shape)],
        out_specs=(row(r), row(LANES), row(hk), row(hk)),
        compiler_params=_cparams(("parallel",)),
        name="mla_latent",
    )(h, g, w, gc, cos_t, sin_t, wk, ek, wv, ones)


def _query_kernel(h_ref, g_ref, wdq_ref, gq_ref, wq_ref, wqr_ref, cos_ref, sin_ref, q_ref, *, scale):
    xn = _rms(h_ref[...], g_ref[...]).astype(BF16)
    cq = _rms(_dot(xn, wdq_ref[...]), gq_ref[...]).astype(BF16)
    q = _dot(cq, wq_ref[...])
    qr = _dot(cq, wqr_ref[...])
    cos = cos_ref[...] * scale
    sin = sin_ref[...] * scale
    for hd in range(q.shape[1] // LANES):
        lanes = slice(hd * LANES, (hd + 1) * LANES)
        q_ref[:, lanes] = (q[:, lanes] * cos + qr[:, lanes] * sin).astype(BF16)


def _queries(h, g, wdq, gq, wq, wqr, cos_t, sin_t, scale, tm, n_p, n_sblk):
    t, d = h.shape
    hk = wq.shape[1]
    return pl.pallas_call(
        functools.partial(_query_kernel, scale=scale),
        out_shape=jax.ShapeDtypeStruct((t, hk), BF16),
        grid=(t // tm,),
        in_specs=[pl.BlockSpec((tm, d), lambda i: (i, 0)), _const_spec((1, d)), _const_spec(wdq.shape),
                  _const_spec(gq.shape), _const_spec(wq.shape), _const_spec(wqr.shape),
                  _tbl_spec(tm, n_p, n_sblk), _tbl_spec(tm, n_p, n_sblk)],
        out_specs=pl.BlockSpec((tm, hk), lambda i: (i, 0)),
        compiler_params=_cparams(("parallel",)),
        name="mla_queries",
    )(h, g, wdq, gq, wq, wqr, cos_t, sin_t)


def _attn_kernel(q_ref, k_ref, v_ref, o_ref, s00, s01, s10, s11, acc0, acc1, m0, m1, *, tq, tk):
    qi = pl.program_id(2)
    half = LANES // 2
    s_refs = ((s00, s01), (s10, s11))
    acc_refs = (acc0, acc1)
    m_refs = (m0, m1)
    lanes = (slice(0, LANES), slice(LANES, 2 * LANES))
    shift = CHUNK.bit_length() - 1

    def produce(hd, slot, j):
        start = pl.multiple_of(j * tk, tk)
        s_refs[hd][slot][...] = _dot_nt(q_ref[:, lanes[hd]], k_ref[pl.ds(start, tk), lanes[hd]])

    def consume(hd, slot, j, masked):
        start = pl.multiple_of(j * tk, tk)
        s = s_refs[hd][slot][...]
        if masked:
            qpos = qi * tq + lax.broadcasted_iota(jnp.int32, (tq, tk), 0)
            kpos = j * tk + lax.broadcasted_iota(jnp.int32, (tq, tk), 1)
            s = jnp.where(lax.shift_right_logical(kpos, shift) <= lax.shift_right_logical(qpos, shift), s, NEG)
        m_old = m_refs[hd][...]
        m_new = jnp.maximum(m_old, jnp.broadcast_to(jnp.max(s, axis=-1, keepdims=True), m_old.shape))
        p = jnp.exp2(s - jnp.concatenate([m_new] * (tk // LANES), axis=1)).astype(BF16)
        acc_refs[hd][...] = (jnp.exp2(m_old - m_new) * acc_refs[hd][...]
                             + _dot(p, v_ref[pl.ds(start, tk), lanes[hd]]))
        m_refs[hd][...] = m_new

    for hd in range(2):
        m_refs[hd][...] = jnp.full(m_refs[hd].shape, NEG, F32)
        acc_refs[hd][...] = jnp.zeros(acc_refs[hd].shape, F32)
        produce(hd, 0, 0)

    assert tq == tk

    def pair(jj, carry):
        a = 2 * jj
        for hd in range(2):
            produce(hd, 1, a + 1)
        for hd in range(2):
            consume(hd, 0, a, False)
        for hd in range(2):
            produce(hd, 0, a + 2)
        for hd in range(2):
            consume(hd, 1, a + 1, False)
        return carry

    lax.fori_loop(0, qi // 2, pair, 0)
    a = 2 * (qi // 2)

    @pl.when(qi % 2 == 0)
    def _():
        for hd in range(2):
            consume(hd, 0, a, True)

    @pl.when(qi % 2 == 1)
    def _():
        for hd in range(2):
            produce(hd, 1, a + 1)
        for hd in range(2):
            consume(hd, 0, a, False)
        for hd in range(2):
            consume(hd, 1, a + 1, True)

    acc_e, acc_o = acc0[...], acc1[...]
    o_e = acc_e / acc_e[:, half:half + 1]
    o_o = acc_o / acc_o[:, 0:1]
    lane = lax.broadcasted_iota(jnp.int32, (tq, LANES), 1)
    o_ref[...] = jnp.where(lane < half, o_e, o_o).astype(BF16)


def _prompt_attention(q, k, v, n_b, seq, n_heads, tq, tk):
    nq = seq // tq
    hv = n_heads * (LANES // 2)
    return pl.pallas_call(
        functools.partial(_attn_kernel, tq=tq, tk=tk),
        out_shape=jax.ShapeDtypeStruct((n_b * seq, hv), BF16),
        grid=(n_b, n_heads // 2, nq),
        in_specs=[pl.BlockSpec((tq, 2 * LANES), lambda b, hp, i: (b * nq + i, hp)),
                  pl.BlockSpec((seq, 2 * LANES), lambda b, hp, i: (b, hp)),
                  pl.BlockSpec((seq, 2 * LANES), lambda b, hp, i: (b, hp))],
        out_specs=pl.BlockSpec((tq, LANES), lambda b, hp, i: (b * nq + i, hp)),
        scratch_shapes=[pltpu.VMEM((tq, tk), F32)] * 4 + [pltpu.VMEM((tq, LANES), F32)] * 2
                       + [pltpu.VMEM((tq, LANES), F32)] * 2,
        compiler_params=_cparams(("parallel", "parallel", "arbitrary")),
        name="prompt_attention",
    )(q, k, v)


def _qfeat_kernel(q_ref, w_ref, o_ref):
    o_ref[...] = _dot(q_ref[...], w_ref[...]).astype(BF16)


def _sample_qfeat(q, wq2f, row0_blk, ts):
    n_heads, _, fw = wq2f.shape
    return pl.pallas_call(
        _qfeat_kernel,
        out_shape=jax.ShapeDtypeStruct((n_heads, ts, fw), BF16),
        grid=(n_heads,),
        in_specs=[pl.BlockSpec((ts, LANES), lambda hd: (row0_blk, hd)),
                  pl.BlockSpec((None, LANES, fw), lambda hd: (hd, 0, 0))],
        out_specs=pl.BlockSpec((None, ts, fw), lambda hd: (hd, 0, 0)),
        compiler_params=_cparams(("parallel",)),
        name="sample_qfeat",
    )(q, wq2f)


def _sample_attn_kernel(qf_ref, cc_ref, ckpe_ref, cn_ref, kn_ref, e_ref, o_ref):
    n_heads, sq, fw = qf_ref.shape
    r = cc_ref.shape[1]
    qf = qf_ref[...].reshape(n_heads * sq, fw)
    qc, qp = qf[:, :r], qf[:, r:]
    kc = cc_ref[...].astype(BF16)
    kp = _dot(ckpe_ref[...].astype(BF16), e_ref[...]).astype(BF16)
    kcn = cn_ref[...].astype(BF16)
    kpn = kn_ref[...].astype(BF16)
    s1 = _dot_nt(qc, kc) + _dot_nt(qp, kp)
    s2 = _dot_nt(qc, kcn) + _dot_nt(qp, kpn)
    m = jnp.maximum(jnp.max(s1, axis=-1, keepdims=True), jnp.max(s2, axis=-1, keepdims=True))
    p1 = jnp.exp2(s1 - m)
    p2 = jnp.exp2(s2 - m)
    l = jnp.sum(p1, axis=-1, keepdims=True) + jnp.sum(p2, axis=-1, keepdims=True)
    o = (_dot(p1.astype(BF16), kc) + _dot(p2.astype(BF16), kcn)) / l
    o_ref[...] = o.astype(BF16).reshape(n_heads, sq, r)


def _sample_attention(qf, cache_c, cache_kpe, c_all, kpe_all, e_place, tp, sq):
    n_heads, ts, fw = qf.shape
    n_bs, past, r = cache_c.shape
    rope = cache_kpe.shape[-1]
    return pl.pallas_call(
        _sample_attn_kernel,
        out_shape=jax.ShapeDtypeStruct((n_heads, ts, r), BF16),
        grid=(n_bs,),
        in_specs=[pl.BlockSpec((n_heads, sq, fw), lambda b: (0, b, 0)),
                  pl.BlockSpec((None, past, r), lambda b: (b, 0, 0)),
                  pl.BlockSpec((None, past, rope), lambda b: (b, 0, 0)),
                  pl.BlockSpec((sq, r), lambda b: (tp // sq + b, 0)),
                  pl.BlockSpec((sq, LANES), lambda b: (tp // sq + b, 0)),
                  _const_spec(e_place.shape)],
        out_specs=pl.BlockSpec((n_heads, sq, r), lambda b: (0, b, 0)),
        compiler_params=_cparams(("parallel",)),
        name="sample_attention",
    )(qf, cache_c, cache_kpe, c_all, kpe_all, e_place)


def _sample_uv_kernel(ol_ref, wv_ref, o_ref):
    o_ref[...] = (_dot(ol_ref[0], wv_ref[0]) + _dot(ol_ref[1], wv_ref[1])).astype(BF16)


def _sample_uv(o_lat, wv_pair):
    n_heads, ts, r = o_lat.shape
    return pl.pallas_call(
        _sample_uv_kernel,
        out_shape=jax.ShapeDtypeStruct((ts, n_heads * (LANES // 2)), BF16),
        grid=(n_heads // 2,),
        in_specs=[pl.BlockSpec((2, ts, r), lambda hp: (hp, 0, 0)),
                  pl.BlockSpec((2, r, LANES), lambda hp: (hp, 0, 0))],
        out_specs=pl.BlockSpec((ts, LANES), lambda hp: (0, hp)),
        compiler_params=_cparams(("parallel",)),
        name="sample_uv",
    )(o_lat, wv_pair)


def _oproj_kernel(op_ref, os_ref, w_ref, h_ref, out_ref, *, n_p):
    o = jnp.where(pl.program_id(0) < n_p, op_ref[...], os_ref[...])
    out_ref[...] = h_ref[...] + _dot(o, w_ref[...])


def _oproj(o_p, o_s, w, h, tm, n_p):
    t, d = h.shape
    hv = o_p.shape[1]
    return pl.pallas_call(
        functools.partial(_oproj_kernel, n_p=n_p),
        out_shape=jax.ShapeDtypeStruct((t, d), F32),
        grid=(t // tm,),
        in_specs=[pl.BlockSpec((tm, hv), lambda i: (jnp.minimum(i, n_p - 1), 0)),
                  pl.BlockSpec((tm, hv), lambda i: (jnp.maximum(i - n_p, 0), 0)),
                  _const_spec(w.shape),
                  pl.BlockSpec((tm, d), lambda i: (i, 0))],
        out_specs=pl.BlockSpec((tm, d), lambda i: (i, 0)),
        compiler_params=_cparams(("parallel",)),
        name="attn_oproj",
    )(o_p, o_s, w, h)


def _rot_cols(w):
    half = w.shape[-1] // 2
    return jnp.concatenate([-w[..., half:], w[..., :half]], axis=-1)


def _moe_schedule(meta, cnt, n_exp, nt_max):
    counts = cnt[0, :n_exp].astype(jnp.int32)
    ntile = (counts + MOE_TILE - 1) // MOE_TILE
    tile_end = jnp.cumsum(ntile)
    seg_start = (tile_end - ntile) * MOE_TILE
    n_tiles = tile_end[-1]
    e = meta[:, 0:TOP_K].astype(jnp.int32)
    rank = meta[:, TOP_K:2 * TOP_K].astype(jnp.int32)
    pos = (seg_start[e] + rank).reshape(-1)
    tid = jnp.minimum(jnp.arange(nt_max, dtype=jnp.int32), n_tiles - 1)
    tile_e = jnp.sum((tid[:, None] >= tile_end[None, :]).astype(jnp.int32), axis=1)
    lo = jnp.concatenate([seg_start + counts, (n_tiles * MOE_TILE).reshape(1)]).astype(jnp.int32)
    hi = jnp.concatenate([tile_end * MOE_TILE, jnp.full((1,), nt_max * MOE_TILE)]).astype(jnp.int32)
    return pos.astype(jnp.int32), tile_e.astype(jnp.int32), lo, hi


def kernel(x_prompt, x_sample, state_conv, cache_ckv, cache_kpe, p_prompt, p_sample, g_mix, g_ffn, w_pw1, b_pw1, w_dw, b_dw, ln_g, ln_b, w_pw2, b_pw2, g_kv, w_dkv, g_ckv, w_uk, w_uv, w_dq, g_q, w_uq, w_o, w_gate, w_up, w_down, w_router, we_gate, we_up, we_down, w_ple, w_ple_gate, g_ple, g_final):
    n_b, seq, d = x_prompt.shape
    n_bs, seq_s, _ = x_sample.shape
    depth = g_mix.shape[0]
    n_a = w_pw1.shape[0]
    taps = w_dw.shape[1]
    past = cache_ckv.shape[1]
    r_kv = cache_ckv.shape[2]
    rope = cache_kpe.shape[2]
    n_heads, nope = w_uk.shape[1], w_uk.shape[2]
    v_dim = w_uv.shape[2]
    n_exp = w_router.shape[2]
    ple = p_prompt.shape[-1]
    tp, ts = n_b * seq, n_bs * seq_s
    t = tp + ts
    tm = ts
    assert tp % tm == 0 and seq % tm == 0 and tm % MOE_TILE == 0
    assert nope == LANES // 2 and v_dim == LANES // 2 and nope + rope <= LANES and taps - 1 <= HALO
    n_p = tp // tm
    n_sblk = seq // tm
    scale = float(nope + rope) ** -0.5 * math.log2(math.e)
    row = lambda a: a.reshape(1, -1)
    bf = lambda a: a.astype(BF16)

    h = jnp.concatenate([x_prompt.reshape(tp, d), x_sample.reshape(ts, d)], axis=0)
    pp = p_prompt.reshape(depth, tp, ple)
    ps = p_sample.reshape(depth, ts, ple)

    posn = jnp.concatenate([jnp.arange(seq), past + (jnp.arange(ts) % seq_s)]).astype(F32)
    inv = ROPE_THETA ** (-jnp.arange(0, rope, 2, dtype=F32) / rope)
    ang = posn[:, None] * inv[None, :]
    ang = jnp.concatenate([ang, ang], axis=-1)
    pad_r = LANES - nope - rope
    cos_t = jnp.concatenate([jnp.ones((posn.shape[0], nope), F32), jnp.cos(ang),
                             jnp.zeros((posn.shape[0], pad_r), F32)], axis=1)
    sin_t = jnp.concatenate([jnp.zeros((posn.shape[0], nope), F32), jnp.sin(ang),
                             jnp.zeros((posn.shape[0], pad_r), F32)], axis=1)

    conv_p, conv_s = [], []
    c_all = kpe_all = k_cat = v_cat = None
    nt_max = (TOP_K * t) // MOE_TILE + n_exp
    rows_pad = nt_max * MOE_TILE

    for i in range(depth):
        if i < n_a:
            a = i
            u = _pw1_glu(h, row(g_mix[i]), bf(w_pw1[a]), row(b_pw1[a]), tm)
            st_pad = jnp.pad(state_conv[a], ((0, 0), (HALO - (taps - 1), 0), (0, 0))).reshape(n_bs * HALO, d)
            h = _conv_block(u, h, st_pad, w_dw[a], row(b_dw[a]), row(ln_g[a]), row(ln_b[a]),
                            bf(w_pw2[a]), row(b_pw2[a]), n_b=n_b, seq=seq, n_bs=n_bs, seq_s=seq_s, tt=tm)
            conv_p.append(jnp.stack([u[(bi + 1) * seq - (taps - 1):(bi + 1) * seq] for bi in range(n_b)]))
            us = u[tp:].reshape(n_bs, seq_s, d)
            conv_s.append(jnp.concatenate([state_conv[a], us], axis=1)[:, -(taps - 1):])
        else:
            b = i - n_a
            wq = w_uq[b]
            zq = jnp.zeros(wq.shape[:2] + (pad_r,), F32)
            wq_cat = jnp.concatenate([wq, zq], axis=-1).reshape(wq.shape[0], n_heads * LANES)
            wq_rot = jnp.concatenate([jnp.zeros(wq.shape[:2] + (nope,), F32), _rot_cols(wq[..., nope:]), zq],
                                     axis=-1).reshape(wq.shape[0], n_heads * LANES)
            q = _queries(h, row(g_mix[i]), bf(w_dq[b]), row(g_q[b]), bf(wq_cat), bf(wq_rot),
                         cos_t, sin_t, scale, tm, n_p, n_sblk)
            o_p = _prompt_attention(q, k_cat, v_cat, n_b, seq, n_heads, tq=tm, tk=tm)
            wk_t = jnp.transpose(w_uk, (1, 2, 0))
            sel = jnp.zeros((LANES, LANES), F32).at[jnp.arange(nope, nope + rope), jnp.arange(nope, nope + rope)].set(1.0)
            wq2f = jnp.concatenate([
                jnp.concatenate([wk_t, jnp.zeros((n_heads, LANES - nope, r_kv), F32)], axis=1),
                jnp.broadcast_to(sel, (n_heads, LANES, LANES))], axis=2)
            qf = _sample_qfeat(q, bf(wq2f), n_p, ts)
            e_place = jnp.zeros((rope, LANES), F32).at[jnp.arange(rope), nope + jnp.arange(rope)].set(1.0)
            o_lat = _sample_attention(qf, cache_ckv, cache_kpe, c_all, kpe_all, bf(e_place), tp, seq_s)
            wv_h = jnp.transpose(w_uv, (1, 0, 2))
            zv = jnp.zeros_like(wv_h)
            wv_pair = jnp.where((jnp.arange(n_heads) % 2 == 0)[:, None, None],
                                jnp.concatenate([wv_h, zv], axis=2), jnp.concatenate([zv, wv_h], axis=2))
            o_s = _sample_uv(o_lat, bf(wv_pair))
            h = _oproj(o_p, o_s, bf(w_o[b].reshape(n_heads * v_dim, d)), h, tm, n_p)

        j = i // 2
        last = i == depth - 1
        if i % 2 == 0:
            h = _ffn_ple(h, row(g_ffn[i]), bf(w_gate[j]), bf(w_up[j]), bf(w_down[j]), pp, ps, i,
                         row(g_ple[i]), bf(w_ple_gate[i]), bf(w_ple[i]), tm, n_p)
        else:
            wr = jnp.pad(w_router[j], ((0, 0), (0, LANES - n_exp)))
            wr_hi = bf(wr)
            wr_lo = bf(wr - wr_hi.astype(F32))
            meta, cnt = _router(h, row(g_ffn[i]), wr_hi, wr_lo, n_exp, tm)
            pos, tile_e, pad_lo, pad_hi = _moe_schedule(meta, cnt, n_exp, nt_max)
            inv = _moe_inverse(pos, pad_lo, pad_hi, rows_pad)
            ys = _moe_perm_ffn(tile_e, inv, h, row(g_ffn[i]), bf(we_gate), bf(we_up), bf(we_down), j)
            h = _merge_ple(h, meta, ys, pp, ps, i, row(g_ple[i]), bf(w_ple_gate[i]), bf(w_ple[i]),
                           row(g_final), tm, n_p, final=last)
        if last and i % 2 == 0:
            raise NotImplementedError("final norm is fused into the MoE combine of the last layer")

        if i == n_a - 1:
            w_ext = jnp.concatenate([
                w_dkv[:, :r_kv],
                jnp.zeros((d, nope), F32), w_dkv[:, r_kv:], jnp.zeros((d, pad_r), F32),
                jnp.zeros((d, nope), F32), _rot_cols(w_dkv[:, r_kv:]), jnp.zeros((d, pad_r), F32)], axis=1)
            wk_cat = jnp.concatenate([w_uk, jnp.zeros((r_kv, n_heads, LANES - nope), F32)], axis=2)
            wk_cat = wk_cat.reshape(r_kv, n_heads * LANES)
            lane = jnp.arange(LANES)
            e_k = ((lane[:, None] == (jnp.arange(n_heads * LANES) % LANES)[None, :])
                   & (lane[:, None] >= nope) & (lane[:, None] < nope + rope)).astype(F32)
            zv = jnp.zeros_like(w_uv)
            wv_cat = jnp.where((jnp.arange(n_heads) % 2 == 0)[None, :, None],
                               jnp.concatenate([w_uv, zv], axis=2), jnp.concatenate([zv, w_uv], axis=2))
            wv_cat = wv_cat.reshape(r_kv, n_heads * LANES)
            col = jnp.arange(n_heads * LANES)
            ones_lane = jnp.where((col // LANES) % 2 == 0, v_dim, 0)
            v_ones = (col % LANES == ones_lane).astype(F32).reshape(1, -1)
            c_all, kpe_all, k_cat, v_cat = _latent(h, row(g_kv), bf(w_ext), row(g_ckv), cos_t, sin_t,
                                                   bf(wk_cat), bf(e_k), bf(wv_cat), v_ones, tm, n_p, n_sblk)

    kpe_out = kpe_all[:, nope:nope + rope]
    y_p, y_s = h
    return (y_p.reshape(n_b, seq, d), y_s.reshape(n_bs, seq_s, d),
            jnp.stack(conv_p), jnp.stack(conv_s),
            c_all[:tp].reshape(n_b, seq, r_kv), kpe_out[:tp].reshape(n_b, seq, rope),
            c_all[tp:].reshape(n_bs, seq_s, r_kv), kpe_out[tp:].reshape(n_bs, seq_s, rope))
```

```python
import functools
import math

import jax
import jax.numpy as jnp
from jax import lax
from jax.experimental import pallas as pl
from jax.experimental.pallas import tpu as pltpu

EPS = 1e-6
NEG = -1e30
CHUNK = 64
ROPE_THETA = 10000.0
TOP_K = 2

LANES = 128
HALO = 32
MXU_COLS = 256
MOE_TILE = 256
ISSUE_UNROLL = 8
VMEM_LIMIT = 60 * 1024 * 1024

F32 = jnp.float32
BF16 = jnp.bfloat16


def _cparams(sem=None):
    return pltpu.CompilerParams(dimension_semantics=sem, vmem_limit_bytes=VMEM_LIMIT)


def _const_spec(shape):
    nd = len(shape)
    return pl.BlockSpec(shape, lambda *_: (0,) * nd, pipeline_mode=pl.Buffered(1))


def _rms(x, g):
    return x * lax.rsqrt(jnp.mean(x * x, axis=-1, keepdims=True) + EPS) * g


def _dot(a, b):
    return jnp.dot(a, b, preferred_element_type=F32)


def _dot_nt(a, b):
    return lax.dot_general(a, b, (((1,), (1,)), ((), ())), preferred_element_type=F32)


def _sigmoid(x):
    return 1.0 / (1.0 + jnp.exp(-x))


def _silu(x):
    return x * _sigmoid(x)


def _ff_chunk(f, cap=1536):
    best = None
    for c in range(LANES, min(f, cap) + 1, LANES):
        if f % c == 0:
            best = c
    return best if best is not None else f


def _pw1_kernel(h_ref, g_ref, w_ref, b_ref, u_ref):
    d = h_ref.shape[1]
    xn = _rms(h_ref[...], g_ref[...]).astype(BF16)
    a = _dot(xn, w_ref[...]) + b_ref[...]
    u_ref[...] = a[:, :d] * _sigmoid(a[:, d:])


def _pw1_glu(h, g, w, b, tm):
    t, d = h.shape
    return pl.pallas_call(
        _pw1_kernel,
        out_shape=jax.ShapeDtypeStruct((t, d), F32),
        grid=(t // tm,),
        in_specs=[pl.BlockSpec((tm, d), lambda i: (i, 0)),
                  _const_spec((1, d)), _const_spec((d, 2 * d)), _const_spec((1, 2 * d))],
        out_specs=pl.BlockSpec((tm, d), lambda i: (i, 0)),
        compiler_params=_cparams(("parallel",)),
        name="pw1_glu",
    )(h, g, w, b)


def _conv_kernel(cur_ref, prev_ref, h_ref, wdw_ref, bdw_ref, lg_ref, lb_ref, w2_ref, b2_ref,
                 o_ref, full_ref, sh_ref, y_ref, *, taps, zero_first):
    tt, d = cur_ref.shape
    nch = d // LANES
    prev = prev_ref[...]
    if zero_first:
        prev = jnp.where(pl.program_id(1) == 0, 0.0, prev)
    for c in range(nch):
        full_ref[c, 0:HALO, :] = prev[:, c * LANES:(c + 1) * LANES]
        full_ref[c, HALO:HALO + tt, :] = cur_ref[:, c * LANES:(c + 1) * LANES]
    off = HALO - (taps - 1)
    n_sh = sh_ref.shape[1]
    rc = min(tt, 32)

    def chunk(c, carry):
        for s in range(1, 8):
            sh_ref[s - 1] = full_ref[c, pl.ds(s, n_sh), :]
        wk = [wdw_ref[c, k:k + 1, :] for k in range(taps)]
        for r in range(tt // rc):
            acc = None
            for k in range(taps):
                j, s = divmod(k + off, 8)
                rows = pl.ds(r * rc + 8 * j, rc)
                term = wk[k] * (full_ref[c, rows, :] if s == 0 else sh_ref[s - 1, rows, :])
                acc = term if acc is None else acc + term
            y_ref[c, r * rc:(r + 1) * rc, :] = acc
        return carry

    lax.fori_loop(0, nch, chunk, 0)
    y = jnp.concatenate([y_ref[c] for c in range(nch)], axis=1) + bdw_ref[...]
    mu = jnp.mean(y, axis=-1, keepdims=True)
    yc = y - mu
    var = jnp.mean(yc * yc, axis=-1, keepdims=True)
    z = _silu(yc * lax.rsqrt(var + EPS) * lg_ref[...] + lb_ref[...])
    o_ref[...] = h_ref[...] + _dot(z.astype(BF16), w2_ref[...]) + b2_ref[...]


def _conv_block(u, h, st_pad, wdw, bdw, lg, lb, w2, b2, *, n_b, seq, n_bs, seq_s, tt):
    t, d = h.shape
    tp = n_b * seq
    taps = wdw.shape[0]
    nblk = seq // tt
    nch = d // LANES
    wdw = jnp.transpose(wdw.reshape(taps, nch, LANES), (1, 0, 2))
    consts = [_const_spec(wdw.shape), _const_spec((1, d)), _const_spec((1, d)), _const_spec((1, d)),
              _const_spec((d, d)), _const_spec((1, d))]
    scratch = lambda rows: [pltpu.VMEM((nch, HALO + rows, LANES), F32),
                            pltpu.VMEM((7, HALO + rows - 8, LANES), F32),
                            pltpu.VMEM((nch, rows, LANES), F32)]
    cur_p = lambda b, i: (b * nblk + i, 0)
    h = pl.pallas_call(
        functools.partial(_conv_kernel, taps=taps, zero_first=True),
        out_shape=jax.ShapeDtypeStruct((t, d), F32),
        grid=(n_b, nblk),
        in_specs=[pl.BlockSpec((tt, d), cur_p),
                  pl.BlockSpec((HALO, d), lambda b, i: (jnp.maximum((b * seq + i * tt) // HALO - 1, 0), 0)),
                  pl.BlockSpec((tt, d), cur_p)] + consts,
        out_specs=pl.BlockSpec((tt, d), cur_p),
        scratch_shapes=scratch(tt),
        input_output_aliases={2: 0},
        compiler_params=_cparams(("parallel", "parallel")),
        name="conv_prompt",
    )(u, u, h, wdw, bdw, lg, lb, w2, b2)
    cur_s = lambda b: (tp // seq_s + b, 0)
    return pl.pallas_call(
        functools.partial(_conv_kernel, taps=taps, zero_first=False),
        out_shape=jax.ShapeDtypeStruct((t, d), F32),
        grid=(n_bs,),
        in_specs=[pl.BlockSpec((seq_s, d), cur_s),
                  pl.BlockSpec((HALO, d), lambda b: (b, 0)),
                  pl.BlockSpec((seq_s, d), cur_s)] + consts,
        out_specs=pl.BlockSpec((seq_s, d), cur_s),
        scratch_shapes=scratch(seq_s),
        input_output_aliases={2: 0},
        compiler_params=_cparams(("parallel",)),
        name="conv_sample",
    )(u, st_pad, h, wdw, bdw, lg, lb, w2, b2)


def _ple(h, p, gp, wpg, wple):
    gate = _sigmoid(_dot(_rms(h, gp).astype(BF16), wpg))
    return h + _dot(p.astype(BF16), wple) * gate


def _p_specs(layer, tm, n_p, ple):
    return [pl.BlockSpec((None, tm, ple), lambda i, *_: (layer, jnp.minimum(i, n_p - 1), 0)),
            pl.BlockSpec((None, tm, ple), lambda i, *_: (layer, jnp.maximum(i - n_p, 0), 0))]


def _ffn_kernel(h_ref, g_ref, wg_ref, wu_ref, wd_ref, pp_ref, ps_ref, gp_ref, wpg_ref, wple_ref,
                o_ref, act_ref, *, n_p, fc):
    h = h_ref[...]
    xn = _rms(h, g_ref[...]).astype(BF16)
    f = wg_ref.shape[1]
    for c in range(f // fc):
        cols = slice(c * fc, (c + 1) * fc)
        act_ref[:, cols] = (_silu(_dot(xn, wg_ref[:, cols])) * _dot(xn, wu_ref[:, cols])).astype(BF16)
    h = h + _dot(act_ref[...], wd_ref[...])
    p = jnp.where(pl.program_id(0) < n_p, pp_ref[...], ps_ref[...])
    o_ref[...] = _ple(h, p, gp_ref[...], wpg_ref[...], wple_ref[...])


def _ffn_ple(h, g, wg, wu, wd, pp, ps, layer, gp, wpg, wple, tm, n_p):
    t, d = h.shape
    f = wg.shape[1]
    ple = pp.shape[-1]
    return pl.pallas_call(
        functools.partial(_ffn_kernel, n_p=n_p, fc=_ff_chunk(f, cap=MXU_COLS)),
        out_shape=jax.ShapeDtypeStruct((t, d), F32),
        grid=(t // tm,),
        in_specs=[pl.BlockSpec((tm, d), lambda i: (i, 0)), _const_spec((1, d)),
                  _const_spec((d, f)), _const_spec((d, f)), _const_spec((f, d))]
                 + _p_specs(layer, tm, n_p, ple)
                 + [_const_spec((1, d)), _const_spec((d, d)), _const_spec((ple, d))],
        out_specs=pl.BlockSpec((tm, d), lambda i: (i, 0)),
        scratch_shapes=[pltpu.VMEM((tm, f), BF16)],
        compiler_params=_cparams(("parallel",)),
        name="ffn_ple",
    )(h, g, wg, wu, wd, pp, ps, gp, wpg, wple)


def _router_kernel(h_ref, g_ref, whi_ref, wlo_ref, meta_ref, cnt_ref, run_ref, *, n_exp):
    i = pl.program_id(0)

    @pl.when(i == 0)
    def _():
        run_ref[...] = jnp.zeros_like(run_ref)

    tm = h_ref.shape[0]
    xn = _rms(h_ref[...], g_ref[...])
    hi = xn.astype(BF16)
    lo = (xn - hi.astype(F32)).astype(BF16)
    logits = _dot(hi, whi_ref[...]) + (_dot(hi, wlo_ref[...]) + _dot(lo, whi_ref[...]))
    lane = lax.broadcasted_iota(jnp.int32, (tm, LANES), 1).astype(F32)
    logits = jnp.where(lane < n_exp, logits, -jnp.inf)
    m1 = jnp.max(logits, axis=-1, keepdims=True)
    i1 = jnp.min(jnp.where(logits == m1, lane, float(LANES)), axis=-1, keepdims=True)
    rest = jnp.where(lane == i1, -jnp.inf, logits)
    m2 = jnp.max(rest, axis=-1, keepdims=True)
    i2 = jnp.min(jnp.where(rest == m2, lane, float(LANES)), axis=-1, keepdims=True)
    e2 = jnp.exp(m2 - m1)
    w1 = 1.0 / (1.0 + e2)
    w2 = e2 / (1.0 + e2)
    oh1 = (lane == i1).astype(F32)
    oh2 = (lane == i2).astype(F32)
    oh = oh1 + oh2
    rows = lax.broadcasted_iota(jnp.int32, (tm, tm), 0)
    cols = lax.broadcasted_iota(jnp.int32, (tm, tm), 1)
    tri = jnp.where(cols < rows, 1.0, 0.0).astype(BF16)
    pre = _dot(tri, oh.astype(BF16)) + run_ref[...]
    r1 = jnp.sum(pre * oh1, axis=-1, keepdims=True)
    r2 = jnp.sum(pre * oh2, axis=-1, keepdims=True)
    run = run_ref[...] + jnp.sum(oh, axis=0, keepdims=True)
    run_ref[...] = run
    cnt_ref[...] = jnp.broadcast_to(run, cnt_ref.shape)
    meta = jnp.where(lane == 0, i1, 0.0)
    meta = jnp.where(lane == 1, i2, meta)
    meta = jnp.where(lane == 2, r1, meta)
    meta = jnp.where(lane == 3, r2, meta)
    meta = jnp.where(lane == 4, w1, meta)
    meta = jnp.where(lane == 5, w2, meta)
    meta_ref[...] = meta


def _router(h, g, whi, wlo, n_exp, tm):
    t, d = h.shape
    return pl.pallas_call(
        functools.partial(_router_kernel, n_exp=n_exp),
        out_shape=(jax.ShapeDtypeStruct((t, LANES), F32), jax.ShapeDtypeStruct((8, LANES), F32)),
        grid=(t // tm,),
        in_specs=[pl.BlockSpec((tm, d), lambda i: (i, 0)), _const_spec((1, d)),
                  _const_spec((d, LANES)), _const_spec((d, LANES))],
        out_specs=(pl.BlockSpec((tm, LANES), lambda i: (i, 0)), pl.BlockSpec((8, LANES), lambda i: (0, 0))),
        scratch_shapes=[pltpu.VMEM((1, LANES), F32)],
        compiler_params=_cparams(("arbitrary",)),
        name="router",
    )(h, g, whi, wlo)


def _inverse_kernel(pos_ref, lo_ref, hi_ref, inv_ref, *, n_rng):
    for e in range(n_rng):
        def fill(r, carry):
            inv_ref[r] = -1
            return carry

        lax.fori_loop(lo_ref[e], hi_ref[e], fill, 0)

    def put(idx, carry):
        inv_ref[pos_ref[idx]] = idx
        return carry

    lax.fori_loop(0, pos_ref.shape[0], put, 0, unroll=ISSUE_UNROLL)


def _moe_inverse(pos, lo, hi, rows):
    return pl.pallas_call(
        functools.partial(_inverse_kernel, n_rng=lo.shape[0]),
        out_shape=jax.ShapeDtypeStruct((rows,), jnp.int32),
        grid_spec=pltpu.PrefetchScalarGridSpec(
            num_scalar_prefetch=3, grid=(1,), in_specs=[],
            out_specs=pl.BlockSpec(memory_space=pltpu.SMEM)),
        compiler_params=_cparams(("arbitrary",)),
        name="moe_inverse",
    )(pos, lo, hi)


def _moe_perm_kernel(te_ref, inv_ref, h_ref, g_ref, wg_ref, wu_ref, wd_ref, o_ref,
                     xg_ref, yb_ref, act_ref, gsem, ssem, zsem, *, fc, t_tok, n_steps):
    i = pl.program_id(0)
    last = n_steps - 1
    trash = TOP_K * t_tok

    def gather_row(tile, slot, r):
        idx = inv_ref[tile * MOE_TILE + r]
        tok = lax.shift_right_logical(jnp.maximum(idx, 0), 1)
        pltpu.make_async_copy(h_ref.at[pl.ds(tok, 1)], xg_ref.at[slot, pl.ds(r, 1)], gsem.at[slot]).start()

    def scatter_row(tile, slot, r):
        idx = inv_ref[tile * MOE_TILE + r]
        dest = jnp.where(idx >= 0, (idx & 1) * t_tok + lax.shift_right_logical(idx, 1), trash + r)
        pltpu.make_async_copy(yb_ref.at[slot, pl.ds(r, 1)], o_ref.at[pl.ds(dest, 1)], ssem.at[slot]).start()

    def rolled(row_fn, tile, slot):
        def body(r, carry):
            row_fn(tile, slot, r)
            return carry

        lax.fori_loop(0, MOE_TILE, body, 0, unroll=ISSUE_UNROLL)

    def wait_gather(slot):
        pltpu.make_async_copy(h_ref.at[pl.ds(0, MOE_TILE)], xg_ref.at[slot], gsem.at[slot]).wait()

    def wait_scatter(slot):
        pltpu.make_async_copy(yb_ref.at[slot], o_ref.at[pl.ds(0, MOE_TILE)], ssem.at[slot]).wait()

    def compute(slot, issue_gathers, issue_scatters):
        f = wg_ref.shape[1]
        n_chunk = f // fc
        share = MOE_TILE // n_chunk
        x = _rms(xg_ref[slot], g_ref[...]).astype(BF16)
        issue_gathers()
        for c in range(n_chunk):
            cols = slice(c * fc, (c + 1) * fc)
            gate = _dot(x, wg_ref[:, cols])
            up = _dot(x, wu_ref[:, cols])
            if c == 0:
                row0 = pl.multiple_of(jnp.minimum(te_ref[0], 0) * 8, 8)
                xg_ref[slot, pl.ds(row0, 8), 0:LANES] = up[0:8, 0:LANES]
            act_ref[:, cols] = (_silu(gate) * up).astype(BF16)
            issue_scatters(c * share, MOE_TILE if c == n_chunk - 1 else (c + 1) * share)
        yb_ref[slot] = _dot(act_ref[...], wd_ref[...])

    @pl.when(i == 0)
    def _():
        yb_ref[1] = jnp.zeros(yb_ref.shape[1:], F32)
        cp = pltpu.make_async_copy(yb_ref.at[1], o_ref.at[pl.ds(trash, MOE_TILE)], zsem)
        cp.start()
        cp.wait()
        rolled(gather_row, 0, 0)
        wait_gather(0)
        def issue_gathers():
            for r in range(MOE_TILE):
                gather_row(1, 1, r)

        compute(0, issue_gathers, lambda r0, r1: None)

    for par in range(2):
        @pl.when((i > 0) & (i % 2 == par))
        def _():
            wait_gather(par)

            @pl.when(i >= 2)
            def _():
                wait_scatter(par)

            nxt = jnp.minimum(i + 1, last)

            def issue_gathers():
                for r in range(MOE_TILE):
                    gather_row(nxt, 1 - par, r)

            def issue_scatters(r0, r1):
                for r in range(r0, r1):
                    scatter_row(i - 1, 1 - par, r)

            compute(par, issue_gathers, issue_scatters)

    @pl.when(i == last)
    def _():
        wait_scatter(1 - last % 2)
        rolled(scatter_row, last, last % 2)
        wait_scatter(last % 2)
        wait_gather(1 - last % 2)


def _moe_perm_ffn(tile_e, inv, h, g, weg, weu, wed, layer):
    t, d = h.shape
    f = weg.shape[-1]
    n_steps = tile_e.shape[0]
    wmap = lambda i, te, inv_: (layer, te[i], 0, 0)
    buf = lambda: pltpu.VMEM((2, MOE_TILE, d), F32)
    return pl.pallas_call(
        functools.partial(_moe_perm_kernel, fc=_ff_chunk(f), t_tok=t, n_steps=n_steps),
        out_shape=jax.ShapeDtypeStruct((TOP_K * t + MOE_TILE, d), F32),
        grid_spec=pltpu.PrefetchScalarGridSpec(
            num_scalar_prefetch=2,
            grid=(n_steps,),
            in_specs=[pl.BlockSpec(memory_space=pl.ANY),
                      pl.BlockSpec((1, d), lambda i, te, inv_: (0, 0)),
                      pl.BlockSpec((None, None, d, f), wmap),
                      pl.BlockSpec((None, None, d, f), wmap),
                      pl.BlockSpec((None, None, f, d), wmap)],
            out_specs=pl.BlockSpec(memory_space=pl.ANY),
            scratch_shapes=[buf(), buf(), pltpu.VMEM((MOE_TILE, f), BF16),
                            pltpu.SemaphoreType.DMA((2,)), pltpu.SemaphoreType.DMA((2,)),
                            pltpu.SemaphoreType.DMA(())]),
        compiler_params=_cparams(("arbitrary",)),
        name="moe_perm_ffn",
    )(tile_e, inv, h, g, weg, weu, wed)


def _merge_kernel(h_ref, meta_ref, y0_ref, y1_ref, pp_ref, ps_ref, gp_ref, wpg_ref, wple_ref, gf_ref,
                  *outs, n_p, final):
    i = pl.program_id(0)
    meta = meta_ref[...]
    h = h_ref[...] + (meta[:, 4:5] * y0_ref[...] + meta[:, 5:6] * y1_ref[...])
    p = jnp.where(i < n_p, pp_ref[...], ps_ref[...])
    h = _ple(h, p, gp_ref[...], wpg_ref[...], wple_ref[...])
    if final:
        h = _rms(h, gf_ref[...])
        op_ref, os_ref = outs

        @pl.when(i < n_p)
        def _():
            op_ref[...] = h

        @pl.when(i >= n_p)
        def _():
            os_ref[...] = h
    else:
        outs[0][...] = h


def _merge_ple(h, meta, ys, pp, ps, layer, gp, wpg, wple, gf, tm, n_p, final):
    t, d = h.shape
    ple = pp.shape[-1]
    row = lambda off: pl.BlockSpec((tm, d), lambda i: (off + i, 0))
    if final:
        out_shape = (jax.ShapeDtypeStruct((n_p * tm, d), F32), jax.ShapeDtypeStruct((t - n_p * tm, d), F32))
        out_specs = (pl.BlockSpec((tm, d), lambda i: (jnp.minimum(i, n_p - 1), 0)),
                     pl.BlockSpec((tm, d), lambda i: (jnp.maximum(i - n_p, 0), 0)))
    else:
        out_shape = jax.ShapeDtypeStruct((t, d), F32)
        out_specs = row(0)
    return pl.pallas_call(
        functools.partial(_merge_kernel, n_p=n_p, final=final),
        out_shape=out_shape,
        grid=(t // tm,),
        in_specs=[row(0), pl.BlockSpec((tm, LANES), lambda i: (i, 0)), row(0), row(t // tm)]
                 + _p_specs(layer, tm, n_p, ple)
                 + [_const_spec((1, d)), _const_spec((d, d)), _const_spec((ple, d)), _const_spec((1, d))],
        out_specs=out_specs,
        compiler_params=_cparams(("arbitrary",)),
        name="moe_merge",
    )(h, meta, ys, ys, pp, ps, gp, wpg, wple, gf)


def _latent_kernel(h_ref, g_ref, w_ref, gc_ref, cos_ref, sin_ref, wk_ref, ek_ref, wv_ref, one_ref,
                   c_ref, kpe_ref, k_ref, v_ref):
    r = c_ref.shape[1]
    xn = _rms(h_ref[...], g_ref[...]).astype(BF16)
    z = _dot(xn, w_ref[...])
    c = _rms(z[:, :r], gc_ref[...])
    kpe = z[:, r:r + LANES] * cos_ref[...] + z[:, r + LANES:r + 2 * LANES] * sin_ref[...]
    c_ref[...] = c
    kpe_ref[...] = kpe
    cb = c.astype(BF16)
    k_ref[...] = (_dot(cb, wk_ref[...]) + _dot(kpe.astype(BF16), ek_ref[...])).astype(BF16)
    v_ref[...] = (_dot(cb, wv_ref[...]) + one_ref[...]).astype(BF16)


def _tbl_spec(tm, n_p, n_sblk):
    return pl.BlockSpec((tm, LANES), lambda i: (jnp.where(i < n_p, i % n_sblk, n_sblk + i - n_p), 0))


def _latent(h, g, w, gc, cos_t, sin_t, wk, ek, wv, ones, tm, n_p, n_sblk):
    t, d = h.shape
    r = gc.shape[1]
    hk = wk.shape[1]
    row = lambda w_: pl.BlockSpec((tm, w_), lambda i: (i, 0))
    return pl.pallas_call(
        _latent_kernel,
        out_shape=(jax.ShapeDtypeStruct((t, r), F32), jax.ShapeDtypeStruct((t, LANES), F32),
                   jax.ShapeDtypeStruct((t, hk), BF16), jax.ShapeDtypeStruct((t, hk), BF16)),
        grid=(t // tm,),
        in_specs=[row(d), _const_spec((1, d)), _const_spec(w.shape), _const_spec((1, r)),
                  _tbl_spec(tm, n_p, n_sblk), _tbl_spec(tm, n_p, n_sblk),
                  _const_spec(wk.shape), _const_spec(ek.shape), _const_spec(wv.shape), _const_spec(ones.shape)],
        out_specs=(row(r), row(LANES), row(hk), row(hk)),
        compiler_params=_cparams(("parallel",)),
        name="mla_latent",
    )(h, g, w, gc, cos_t, sin_t, wk, ek, wv, ones)


def _query_kernel(h_ref, g_ref, wdq_ref, gq_ref, wq_ref, wqr_ref, cos_ref, sin_ref, q_ref, *, scale):
    xn = _rms(h_ref[...], g_ref[...]).astype(BF16)
    cq = _rms(_dot(xn, wdq_ref[...]), gq_ref[...]).astype(BF16)
    q = _dot(cq, wq_ref[...])
    qr = _dot(cq, wqr_ref[...])
    cos = cos_ref[...] * scale
    sin = sin_ref[...] * scale
    for hd in range(q.shape[1] // LANES):
        lanes = slice(hd * LANES, (hd + 1) * LANES)
        q_ref[:, lanes] = (q[:, lanes] * cos + qr[:, lanes] * sin).astype(BF16)


def _queries(h, g, wdq, gq, wq, wqr, cos_t, sin_t, scale, tm, n_p, n_sblk):
    t, d = h.shape
    hk = wq.shape[1]
    return pl.pallas_call(
        functools.partial(_query_kernel, scale=scale),
        out_shape=jax.ShapeDtypeStruct((t, hk), BF16),
        grid=(t // tm,),
        in_specs=[pl.BlockSpec((tm, d), lambda i: (i, 0)), _const_spec((1, d)), _const_spec(wdq.shape),
                  _const_spec(gq.shape), _const_spec(wq.shape), _const_spec(wqr.shape),
                  _tbl_spec(tm, n_p, n_sblk), _tbl_spec(tm, n_p, n_sblk)],
        out_specs=pl.BlockSpec((tm, hk), lambda i: (i, 0)),
        compiler_params=_cparams(("parallel",)),
        name="mla_queries",
    )(h, g, wdq, gq, wq, wqr, cos_t, sin_t)


def _attn_kernel(q_ref, k_ref, v_ref, o_ref, *scratch, tq, tk):
    def q_block(qi, carry):
        rows = pl.ds(pl.multiple_of(qi * tq, tq), tq)
        _attn_q_block(qi, q_ref.at[rows], k_ref, v_ref, o_ref.at[rows], *scratch, tq=tq, tk=tk)
        return carry

    lax.fori_loop(0, q_ref.shape[0] // tq, q_block, 0)


def _attn_q_block(qi, q_ref, k_ref, v_ref, o_ref, s00, s01, s10, s11, acc0, acc1, m0, m1, *, tq, tk):
    half = LANES // 2
    s_refs = ((s00, s01), (s10, s11))
    acc_refs = (acc0, acc1)
    m_refs = (m0, m1)
    lanes = (slice(0, LANES), slice(LANES, 2 * LANES))
    shift = CHUNK.bit_length() - 1

    def produce(hd, slot, j):
        start = pl.multiple_of(j * tk, tk)
        s_refs[hd][slot][...] = _dot_nt(q_ref[:, lanes[hd]], k_ref[pl.ds(start, tk), lanes[hd]])

    def consume(hd, slot, j, masked):
        start = pl.multiple_of(j * tk, tk)
        s = s_refs[hd][slot][...]
        if masked:
            qpos = qi * tq + lax.broadcasted_iota(jnp.int32, (tq, tk), 0)
            kpos = j * tk + lax.broadcasted_iota(jnp.int32, (tq, tk), 1)
            s = jnp.where(lax.shift_right_logical(kpos, shift) <= lax.shift_right_logical(qpos, shift), s, NEG)
        m_old = m_refs[hd][...]
        m_new = jnp.maximum(m_old, jnp.broadcast_to(jnp.max(s, axis=-1, keepdims=True), m_old.shape))
        p = jnp.exp2(s - jnp.concatenate([m_new] * (tk // LANES), axis=1)).astype(BF16)
        acc_refs[hd][...] = (jnp.exp2(m_old - m_new) * acc_refs[hd][...]
                             + _dot(p, v_ref[pl.ds(start, tk), lanes[hd]]))
        m_refs[hd][...] = m_new

    for hd in range(2):
        m_refs[hd][...] = jnp.full(m_refs[hd].shape, NEG, F32)
        acc_refs[hd][...] = jnp.zeros(acc_refs[hd].shape, F32)
        produce(hd, 0, 0)

    assert tq == tk

    def pair(jj, carry):
        a = 2 * jj
        for hd in range(2):
            produce(hd, 1, a + 1)
        for hd in range(2):
            consume(hd, 0, a, False)
        for hd in range(2):
            produce(hd, 0, a + 2)
        for hd in range(2):
            consume(hd, 1, a + 1, False)
        return carry

    lax.fori_loop(0, qi // 2, pair, 0)
    a = 2 * (qi // 2)

    @pl.when(qi % 2 == 0)
    def _():
        for hd in range(2):
            consume(hd, 0, a, True)

    @pl.when(qi % 2 == 1)
    def _():
        for hd in range(2):
            produce(hd, 1, a + 1)
        for hd in range(2):
            consume(hd, 0, a, False)
        for hd in range(2):
            consume(hd, 1, a + 1, True)

    acc_e, acc_o = acc0[...], acc1[...]
    o_e = acc_e / acc_e[:, half:half + 1]
    o_o = acc_o / acc_o[:, 0:1]
    lane = lax.broadcasted_iota(jnp.int32, (tq, LANES), 1)
    o_ref[...] = jnp.where(lane < half, o_e, o_o).astype(BF16)


def _prompt_attention(q, k, v, n_b, seq, n_heads, tq, tk):
    hv = n_heads * (LANES // 2)
    return pl.pallas_call(
        functools.partial(_attn_kernel, tq=tq, tk=tk),
        out_shape=jax.ShapeDtypeStruct((n_b * seq, hv), BF16),
        grid=(n_b, n_heads // 2),
        in_specs=[pl.BlockSpec((seq, 2 * LANES), lambda b, hp: (b, hp))] * 3,
        out_specs=pl.BlockSpec((seq, LANES), lambda b, hp: (b, hp)),
        scratch_shapes=[pltpu.VMEM((tq, tk), F32)] * 4 + [pltpu.VMEM((tq, LANES), F32)] * 2
                       + [pltpu.VMEM((tq, LANES), F32)] * 2,
        compiler_params=_cparams(("parallel", "parallel")),
        name="prompt_attention",
    )(q, k, v)


def _qfeat_kernel(q_ref, w_ref, o_ref):
    o_ref[...] = _dot(q_ref[...], w_ref[...]).astype(BF16)


def _sample_qfeat(q, wq2f, row0_blk, ts):
    n_heads, _, fw = wq2f.shape
    return pl.pallas_call(
        _qfeat_kernel,
        out_shape=jax.ShapeDtypeStruct((n_heads, ts, fw), BF16),
        grid=(n_heads,),
        in_specs=[pl.BlockSpec((ts, LANES), lambda hd: (row0_blk, hd)),
                  pl.BlockSpec((None, LANES, fw), lambda hd: (hd, 0, 0))],
        out_specs=pl.BlockSpec((None, ts, fw), lambda hd: (hd, 0, 0)),
        compiler_params=_cparams(("parallel",)),
        name="sample_qfeat",
    )(q, wq2f)


def _sample_attn_kernel(qf_ref, cc_ref, ckpe_ref, cn_ref, kn_ref, e_ref, o_ref):
    n_heads, sq, fw = qf_ref.shape
    r = cc_ref.shape[1]
    qf = qf_ref[...].reshape(n_heads * sq, fw)
    qc, qp = qf[:, :r], qf[:, r:]
    kc = cc_ref[...].astype(BF16)
    kp = _dot(ckpe_ref[...].astype(BF16), e_ref[...]).astype(BF16)
    kcn = cn_ref[...].astype(BF16)
    kpn = kn_ref[...].astype(BF16)
    s1 = _dot_nt(qc, kc) + _dot_nt(qp, kp)
    s2 = _dot_nt(qc, kcn) + _dot_nt(qp, kpn)
    m = jnp.maximum(jnp.max(s1, axis=-1, keepdims=True), jnp.max(s2, axis=-1, keepdims=True))
    p1 = jnp.exp2(s1 - m)
    p2 = jnp.exp2(s2 - m)
    l = jnp.sum(p1, axis=-1, keepdims=True) + jnp.sum(p2, axis=-1, keepdims=True)
    o = (_dot(p1.astype(BF16), kc) + _dot(p2.astype(BF16), kcn)) / l
    o_ref[...] = o.astype(BF16).reshape(n_heads, sq, r)


def _sample_attention(qf, cache_c, cache_kpe, c_all, kpe_all, e_place, tp, sq):
    n_heads, ts, fw = qf.shape
    n_bs, past, r = cache_c.shape
    rope = cache_kpe.shape[-1]
    return pl.pallas_call(
        _sample_attn_kernel,
        out_shape=jax.ShapeDtypeStruct((n_heads, ts, r), BF16),
        grid=(n_bs,),
        in_specs=[pl.BlockSpec((n_heads, sq, fw), lambda b: (0, b, 0)),
                  pl.BlockSpec((None, past, r), lambda b: (b, 0, 0)),
                  pl.BlockSpec((None, past, rope), lambda b: (b, 0, 0)),
                  pl.BlockSpec((sq, r), lambda b: (tp // sq + b, 0)),
                  pl.BlockSpec((sq, LANES), lambda b: (tp // sq + b, 0)),
                  _const_spec(e_place.shape)],
        out_specs=pl.BlockSpec((n_heads, sq, r), lambda b: (0, b, 0)),
        compiler_params=_cparams(("parallel",)),
        name="sample_attention",
    )(qf, cache_c, cache_kpe, c_all, kpe_all, e_place)


def _sample_uv_kernel(ol_ref, wv_ref, o_ref):
    o_ref[...] = (_dot(ol_ref[0], wv_ref[0]) + _dot(ol_ref[1], wv_ref[1])).astype(BF16)


def _sample_uv(o_lat, wv_pair):
    n_heads, ts, r = o_lat.shape
    return pl.pallas_call(
        _sample_uv_kernel,
        out_shape=jax.ShapeDtypeStruct((ts, n_heads * (LANES // 2)), BF16),
        grid=(n_heads // 2,),
        in_specs=[pl.BlockSpec((2, ts, r), lambda hp: (hp, 0, 0)),
                  pl.BlockSpec((2, r, LANES), lambda hp: (hp, 0, 0))],
        out_specs=pl.BlockSpec((ts, LANES), lambda hp: (0, hp)),
        compiler_params=_cparams(("parallel",)),
        name="sample_uv",
    )(o_lat, wv_pair)


def _oproj_kernel(op_ref, os_ref, w_ref, h_ref, out_ref, *, n_p):
    o = jnp.where(pl.program_id(0) < n_p, op_ref[...], os_ref[...])
    out_ref[...] = h_ref[...] + _dot(o, w_ref[...])


def _oproj(o_p, o_s, w, h, tm, n_p):
    t, d = h.shape
    hv = o_p.shape[1]
    return pl.pallas_call(
        functools.partial(_oproj_kernel, n_p=n_p),
        out_shape=jax.ShapeDtypeStruct((t, d), F32),
        grid=(t // tm,),
        in_specs=[pl.BlockSpec((tm, hv), lambda i: (jnp.minimum(i, n_p - 1), 0)),
                  pl.BlockSpec((tm, hv), lambda i: (jnp.maximum(i - n_p, 0), 0)),
                  _const_spec(w.shape),
                  pl.BlockSpec((tm, d), lambda i: (i, 0))],
        out_specs=pl.BlockSpec((tm, d), lambda i: (i, 0)),
        compiler_params=_cparams(("parallel",)),
        name="attn_oproj",
    )(o_p, o_s, w, h)


def _rot_cols(w):
    half = w.shape[-1] // 2
    return jnp.concatenate([-w[..., half:], w[..., :half]], axis=-1)


def _moe_schedule(meta, cnt, n_exp, nt_max):
    counts = cnt[0, :n_exp].astype(jnp.int32)
    ntile = (counts + MOE_TILE - 1) // MOE_TILE
    tile_end = jnp.cumsum(ntile)
    seg_start = (tile_end - ntile) * MOE_TILE
    n_tiles = tile_end[-1]
    e = meta[:, 0:TOP_K].astype(jnp.int32)
    rank = meta[:, TOP_K:2 * TOP_K].astype(jnp.int32)
    pos = (seg_start[e] + rank).reshape(-1)
    tid = jnp.minimum(jnp.arange(nt_max, dtype=jnp.int32), n_tiles - 1)
    tile_e = jnp.sum((tid[:, None] >= tile_end[None, :]).astype(jnp.int32), axis=1)
    lo = jnp.concatenate([seg_start + counts, (n_tiles * MOE_TILE).reshape(1)]).astype(jnp.int32)
    hi = jnp.concatenate([tile_end * MOE_TILE, jnp.full((1,), nt_max * MOE_TILE)]).astype(jnp.int32)
    return pos.astype(jnp.int32), tile_e.astype(jnp.int32), lo, hi


def kernel(x_prompt, x_sample, state_conv, cache_ckv, cache_kpe, p_prompt, p_sample, g_mix, g_ffn, w_pw1, b_pw1, w_dw, b_dw, ln_g, ln_b, w_pw2, b_pw2, g_kv, w_dkv, g_ckv, w_uk, w_uv, w_dq, g_q, w_uq, w_o, w_gate, w_up, w_down, w_router, we_gate, we_up, we_down, w_ple, w_ple_gate, g_ple, g_final):
    n_b, seq, d = x_prompt.shape
    n_bs, seq_s, _ = x_sample.shape
    depth = g_mix.shape[0]
    n_a = w_pw1.shape[0]
    taps = w_dw.shape[1]
    past = cache_ckv.shape[1]
    r_kv = cache_ckv.shape[2]
    rope = cache_kpe.shape[2]
    n_heads, nope = w_uk.shape[1], w_uk.shape[2]
    v_dim = w_uv.shape[2]
    n_exp = w_router.shape[2]
    ple = p_prompt.shape[-1]
    tp, ts = n_b * seq, n_bs * seq_s
    t = tp + ts
    tm = ts
    assert tp % tm == 0 and seq % tm == 0 and tm % MOE_TILE == 0
    assert nope == LANES // 2 and v_dim == LANES // 2 and nope + rope <= LANES and taps - 1 <= HALO
    n_p = tp // tm
    n_sblk = seq // tm
    scale = float(nope + rope) ** -0.5 * math.log2(math.e)
    row = lambda a: a.reshape(1, -1)
    bf = lambda a: a.astype(BF16)

    h = jnp.concatenate([x_prompt.reshape(tp, d), x_sample.reshape(ts, d)], axis=0)
    pp = p_prompt.reshape(depth, tp, ple)
    ps = p_sample.reshape(depth, ts, ple)

    posn = jnp.concatenate([jnp.arange(seq), past + (jnp.arange(ts) % seq_s)]).astype(F32)
    inv = ROPE_THETA ** (-jnp.arange(0, rope, 2, dtype=F32) / rope)
    ang = posn[:, None] * inv[None, :]
    ang = jnp.concatenate([ang, ang], axis=-1)
    pad_r = LANES - nope - rope
    cos_t = jnp.concatenate([jnp.ones((posn.shape[0], nope), F32), jnp.cos(ang),
                             jnp.zeros((posn.shape[0], pad_r), F32)], axis=1)
    sin_t = jnp.concatenate([jnp.zeros((posn.shape[0], nope), F32), jnp.sin(ang),
                             jnp.zeros((posn.shape[0], pad_r), F32)], axis=1)

    conv_p, conv_s = [], []
    c_all = kpe_all = k_cat = v_cat = None
    nt_max = (TOP_K * t) // MOE_TILE + n_exp
    rows_pad = nt_max * MOE_TILE

    for i in range(depth):
        if i < n_a:
            a = i
            u = _pw1_glu(h, row(g_mix[i]), bf(w_pw1[a]), row(b_pw1[a]), tm)
            st_pad = jnp.pad(state_conv[a], ((0, 0), (HALO - (taps - 1), 0), (0, 0))).reshape(n_bs * HALO, d)
            h = _conv_block(u, h, st_pad, w_dw[a], row(b_dw[a]), row(ln_g[a]), row(ln_b[a]),
                            bf(w_pw2[a]), row(b_pw2[a]), n_b=n_b, seq=seq, n_bs=n_bs, seq_s=seq_s, tt=tm)
            conv_p.append(jnp.stack([u[(bi + 1) * seq - (taps - 1):(bi + 1) * seq] for bi in range(n_b)]))
            us = u[tp:].reshape(n_bs, seq_s, d)
            conv_s.append(jnp.concatenate([state_conv[a], us], axis=1)[:, -(taps - 1):])
        else:
            b = i - n_a
            wq = w_uq[b]
            zq = jnp.zeros(wq.shape[:2] + (pad_r,), F32)
            wq_cat = jnp.concatenate([wq, zq], axis=-1).reshape(wq.shape[0], n_heads * LANES)
            wq_rot = jnp.concatenate([jnp.zeros(wq.shape[:2] + (nope,), F32), _rot_cols(wq[..., nope:]), zq],
                                     axis=-1).reshape(wq.shape[0], n_heads * LANES)
            q = _queries(h, row(g_mix[i]), bf(w_dq[b]), row(g_q[b]), bf(wq_cat), bf(wq_rot),
                         cos_t, sin_t, scale, tm, n_p, n_sblk)
            o_p = _prompt_attention(q, k_cat, v_cat, n_b, seq, n_heads, tq=tm, tk=tm)
            wk_t = jnp.transpose(w_uk, (1, 2, 0))
            sel = jnp.zeros((LANES, LANES), F32).at[jnp.arange(nope, nope + rope), jnp.arange(nope, nope + rope)].set(1.0)
            wq2f = jnp.concatenate([
                jnp.concatenate([wk_t, jnp.zeros((n_heads, LANES - nope, r_kv), F32)], axis=1),
                jnp.broadcast_to(sel, (n_heads, LANES, LANES))], axis=2)
            qf = _sample_qfeat(q, bf(wq2f), n_p, ts)
            e_place = jnp.zeros((rope, LANES), F32).at[jnp.arange(rope), nope + jnp.arange(rope)].set(1.0)
            o_lat = _sample_attention(qf, cache_ckv, cache_kpe, c_all, kpe_all, bf(e_place), tp, seq_s)
            wv_h = jnp.transpose(w_uv, (1, 0, 2))
            zv = jnp.zeros_like(wv_h)
            wv_pair = jnp.where((jnp.arange(n_heads) % 2 == 0)[:, None, None],
                                jnp.concatenate([wv_h, zv], axis=2), jnp.concatenate([zv, wv_h], axis=2))
            o_s = _sample_uv(o_lat, bf(wv_pair))
            h = _oproj(o_p, o_s, bf(w_o[b].reshape(n_heads * v_dim, d)), h, tm, n_p)

        j = i // 2
        last = i == depth - 1
        if i % 2 == 0:
            h = _ffn_ple(h, row(g_ffn[i]), bf(w_gate[j]), bf(w_up[j]), bf(w_down[j]), pp, ps, i,
                         row(g_ple[i]), bf(w_ple_gate[i]), bf(w_ple[i]), tm, n_p)
        else:
            wr = jnp.pad(w_router[j], ((0, 0), (0, LANES - n_exp)))
            wr_hi = bf(wr)
            wr_lo = bf(wr - wr_hi.astype(F32))
            meta, cnt = _router(h, row(g_ffn[i]), wr_hi, wr_lo, n_exp, tm)
            pos, tile_e, pad_lo, pad_hi = _moe_schedule(meta, cnt, n_exp, nt_max)
            inv = _moe_inverse(pos, pad_lo, pad_hi, rows_pad)
            ys = _moe_perm_ffn(tile_e, inv, h, row(g_ffn[i]), bf(we_gate), bf(we_up), bf(we_down), j)
            h = _merge_ple(h, meta, ys, pp, ps, i, row(g_ple[i]), bf(w_ple_gate[i]), bf(w_ple[i]),
                           row(g_final), tm, n_p, final=last)
        if last and i % 2 == 0:
            raise NotImplementedError("final norm is fused into the MoE combine of the last layer")

        if i == n_a - 1:
            w_ext = jnp.concatenate([
                w_dkv[:, :r_kv],
                jnp.zeros((d, nope), F32), w_dkv[:, r_kv:], jnp.zeros((d, pad_r), F32),
                jnp.zeros((d, nope), F32), _rot_cols(w_dkv[:, r_kv:]), jnp.zeros((d, pad_r), F32)], axis=1)
            wk_cat = jnp.concatenate([w_uk, jnp.zeros((r_kv, n_heads, LANES - nope), F32)], axis=2)
            wk_cat = wk_cat.reshape(r_kv, n_heads * LANES)
            lane = jnp.arange(LANES)
            e_k = ((lane[:, None] == (jnp.arange(n_heads * LANES) % LANES)[None, :])
                   & (lane[:, None] >= nope) & (lane[:, None] < nope + rope)).astype(F32)
            zv = jnp.zeros_like(w_uv)
            wv_cat = jnp.where((jnp.arange(n_heads) % 2 == 0)[None, :, None],
                               jnp.concatenate([w_uv, zv], axis=2), jnp.concatenate([zv, w_uv], axis=2))
            wv_cat = wv_cat.reshape(r_kv, n_heads * LANES)
            col = jnp.arange(n_heads * LANES)
            ones_lane = jnp.where((col // LANES) % 2 == 0, v_dim, 0)
            v_ones = (col % LANES == ones_lane).astype(F32).reshape(1, -1)
            c_all, kpe_all, k_cat, v_cat = _latent(h, row(g_kv), bf(w_ext), row(g_ckv), cos_t, sin_t,
                                                   bf(wk_cat), bf(e_k), bf(wv_cat), v_ones, tm, n_p, n_sblk)

    kpe_out = kpe_all[:, nope:nope + rope]
    y_p, y_s = h
    return (y_p.reshape(n_b, seq, d), y_s.reshape(n_bs, seq_s, d),
            jnp.stack(conv_p), jnp.stack(conv_s),
            c_all[:tp].reshape(n_b, seq, r_kv), kpe_out[:tp].reshape(n_b, seq, rope),
            c_all[tp:].reshape(n_bs, seq_s, r_kv), kpe_out[tp:].reshape(n_bs, seq_s, rope))
```

```python
import functools
import math

import jax
import jax.numpy as jnp
from jax import lax
from jax.experimental import pallas as pl
from jax.experimental.pallas import tpu as pltpu

EPS = 1e-6
NEG = -1e30
CHUNK = 64
ROPE_THETA = 10000.0
TOP_K = 2

LANES = 128
HALO = 32
MOE_TILE = 256
ISSUE_UNROLL = 8
VMEM_LIMIT = 60 * 1024 * 1024

F32 = jnp.float32
BF16 = jnp.bfloat16


def _cparams(sem=None):
    return pltpu.CompilerParams(dimension_semantics=sem, vmem_limit_bytes=VMEM_LIMIT)


def _const_spec(shape):
    nd = len(shape)
    return pl.BlockSpec(shape, lambda *_: (0,) * nd, pipeline_mode=pl.Buffered(1))


def _rms(x, g):
    return x * lax.rsqrt(jnp.mean(x * x, axis=-1, keepdims=True) + EPS) * g


def _dot(a, b):
    return jnp.dot(a, b, preferred_element_type=F32)


def _dot_nt(a, b):
    return lax.dot_general(a, b, (((1,), (1,)), ((), ())), preferred_element_type=F32)


def _sigmoid(x):
    return 1.0 / (1.0 + jnp.exp(-x))


def _silu(x):
    return x * _sigmoid(x)


def _ff_chunk(f, cap=1536):
    best = None
    for c in range(LANES, min(f, cap) + 1, LANES):
        if f % c == 0:
            best = c
    return best if best is not None else f


def _pw1_kernel(h_ref, g_ref, w_ref, b_ref, u_ref):
    d = h_ref.shape[1]
    xn = _rms(h_ref[...], g_ref[...]).astype(BF16)
    a = _dot(xn, w_ref[...]) + b_ref[...]
    u_ref[...] = a[:, :d] * _sigmoid(a[:, d:])


def _pw1_glu(h, g, w, b, tm):
    t, d = h.shape
    return pl.pallas_call(
        _pw1_kernel,
        out_shape=jax.ShapeDtypeStruct((t, d), F32),
        grid=(t // tm,),
        in_specs=[pl.BlockSpec((tm, d), lambda i: (i, 0)),
                  _const_spec((1, d)), _const_spec((d, 2 * d)), _const_spec((1, 2 * d))],
        out_specs=pl.BlockSpec((tm, d), lambda i: (i, 0)),
        compiler_params=_cparams(("parallel",)),
        name="pw1_glu",
    )(h, g, w, b)


def _conv_kernel(cur_ref, prev_ref, h_ref, wdw_ref, bdw_ref, lg_ref, lb_ref, w2_ref, b2_ref,
                 o_ref, full_ref, sh_ref, y_ref, *, taps, zero_first):
    tt, d = cur_ref.shape
    nch = d // LANES
    prev = prev_ref[...]
    if zero_first:
        prev = jnp.where(pl.program_id(1) == 0, 0.0, prev)
    for c in range(nch):
        full_ref[c, 0:HALO, :] = prev[:, c * LANES:(c + 1) * LANES]
        full_ref[c, HALO:HALO + tt, :] = cur_ref[:, c * LANES:(c + 1) * LANES]
    off = HALO - (taps - 1)
    n_sh = sh_ref.shape[1]
    rc = min(tt, 32)

    def chunk(c, carry):
        for s in range(1, 8):
            sh_ref[s - 1] = full_ref[c, pl.ds(s, n_sh), :]
        wk = [wdw_ref[c, k:k + 1, :] for k in range(taps)]
        for r in range(tt // rc):
            acc = None
            for k in range(taps):
                j, s = divmod(k + off, 8)
                rows = pl.ds(r * rc + 8 * j, rc)
                term = wk[k] * (full_ref[c, rows, :] if s == 0 else sh_ref[s - 1, rows, :])
                acc = term if acc is None else acc + term
            y_ref[c, r * rc:(r + 1) * rc, :] = acc
        return carry

    lax.fori_loop(0, nch, chunk, 0)
    y = jnp.concatenate([y_ref[c] for c in range(nch)], axis=1) + bdw_ref[...]
    mu = jnp.mean(y, axis=-1, keepdims=True)
    yc = y - mu
    var = jnp.mean(yc * yc, axis=-1, keepdims=True)
    z = _silu(yc * lax.rsqrt(var + EPS) * lg_ref[...] + lb_ref[...])
    o_ref[...] = h_ref[...] + _dot(z.astype(BF16), w2_ref[...]) + b2_ref[...]


def _conv_block(u, h, st_pad, wdw, bdw, lg, lb, w2, b2, *, n_b, seq, n_bs, seq_s, tt):
    t, d = h.shape
    tp = n_b * seq
    taps = wdw.shape[0]
    nblk = seq // tt
    nch = d // LANES
    wdw = jnp.transpose(wdw.reshape(taps, nch, LANES), (1, 0, 2))
    consts = [_const_spec(wdw.shape), _const_spec((1, d)), _const_spec((1, d)), _const_spec((1, d)),
              _const_spec((d, d)), _const_spec((1, d))]
    scratch = lambda rows: [pltpu.VMEM((nch, HALO + rows, LANES), F32),
                            pltpu.VMEM((7, HALO + rows - 8, LANES), F32),
                            pltpu.VMEM((nch, rows, LANES), F32)]
    cur_p = lambda b, i: (b * nblk + i, 0)
    h = pl.pallas_call(
        functools.partial(_conv_kernel, taps=taps, zero_first=True),
        out_shape=jax.ShapeDtypeStruct((t, d), F32),
        grid=(n_b, nblk),
        in_specs=[pl.BlockSpec((tt, d), cur_p),
                  pl.BlockSpec((HALO, d), lambda b, i: (jnp.maximum((b * seq + i * tt) // HALO - 1, 0), 0)),
                  pl.BlockSpec((tt, d), cur_p)] + consts,
        out_specs=pl.BlockSpec((tt, d), cur_p),
        scratch_shapes=scratch(tt),
        input_output_aliases={2: 0},
        compiler_params=_cparams(("parallel", "parallel")),
        name="conv_prompt",
    )(u, u, h, wdw, bdw, lg, lb, w2, b2)
    cur_s = lambda b: (tp // seq_s + b, 0)
    return pl.pallas_call(
        functools.partial(_conv_kernel, taps=taps, zero_first=False),
        out_shape=jax.ShapeDtypeStruct((t, d), F32),
        grid=(n_bs,),
        in_specs=[pl.BlockSpec((seq_s, d), cur_s),
                  pl.BlockSpec((HALO, d), lambda b: (b, 0)),
                  pl.BlockSpec((seq_s, d), cur_s)] + consts,
        out_specs=pl.BlockSpec((seq_s, d), cur_s),
        scratch_shapes=scratch(seq_s),
        input_output_aliases={2: 0},
        compiler_params=_cparams(("parallel",)),
        name="conv_sample",
    )(u, st_pad, h, wdw, bdw, lg, lb, w2, b2)


def _ple(h, p, gp, wpg, wple):
    gate = _sigmoid(_dot(_rms(h, gp).astype(BF16), wpg))
    return h + _dot(p.astype(BF16), wple) * gate


def _p_specs(layer, tm, n_p, ple):
    return [pl.BlockSpec((None, tm, ple), lambda i, *_: (layer, jnp.minimum(i, n_p - 1), 0)),
            pl.BlockSpec((None, tm, ple), lambda i, *_: (layer, jnp.maximum(i - n_p, 0), 0))]


def _ffn_kernel(h_ref, g_ref, wg_ref, wu_ref, wd_ref, pp_ref, ps_ref, gp_ref, wpg_ref, wple_ref,
                *rest, n_p, fc, attn):
    o_ref, act_ref = rest[-2:]
    h = h_ref[...]
    if attn:
        ap_ref, as_ref, wo_ref = rest[:3]
        h = h + _dot(jnp.where(pl.program_id(0) < n_p, ap_ref[...], as_ref[...]), wo_ref[...])
    xn = _rms(h, g_ref[...]).astype(BF16)
    f = wg_ref.shape[1]
    for c in range(f // fc):
        cols = slice(c * fc, (c + 1) * fc)
        act_ref[:, cols] = (_silu(_dot(xn, wg_ref[:, cols])) * _dot(xn, wu_ref[:, cols])).astype(BF16)
    h = h + _dot(act_ref[...], wd_ref[...])
    p = jnp.where(pl.program_id(0) < n_p, pp_ref[...], ps_ref[...])
    o_ref[...] = _ple(h, p, gp_ref[...], wpg_ref[...], wple_ref[...])


def _ffn_ple(h, g, wg, wu, wd, pp, ps, layer, gp, wpg, wple, tm, n_p, attn=None):
    t, d = h.shape
    f = wg.shape[1]
    ple = pp.shape[-1]
    extra, extra_specs = (), []
    if attn is not None:
        hv = attn[0].shape[1]
        extra = tuple(attn)
        extra_specs = [pl.BlockSpec((tm, hv), lambda i: (jnp.minimum(i, n_p - 1), 0)),
                       pl.BlockSpec((tm, hv), lambda i: (jnp.maximum(i - n_p, 0), 0)),
                       _const_spec(attn[2].shape)]
    return pl.pallas_call(
        functools.partial(_ffn_kernel, n_p=n_p, fc=_ff_chunk(f, cap=MOE_TILE),
                          attn=attn is not None),
        out_shape=jax.ShapeDtypeStruct((t, d), F32),
        grid=(t // tm,),
        in_specs=[pl.BlockSpec((tm, d), lambda i: (i, 0)), _const_spec((1, d)),
                  _const_spec((d, f)), _const_spec((d, f)), _const_spec((f, d))]
                 + _p_specs(layer, tm, n_p, ple)
                 + [_const_spec((1, d)), _const_spec((d, d)), _const_spec((ple, d))] + extra_specs,
        out_specs=pl.BlockSpec((tm, d), lambda i: (i, 0)),
        scratch_shapes=[pltpu.VMEM((tm, f), BF16)],
        compiler_params=_cparams(("parallel",)),
        name="ffn_ple",
    )(h, g, wg, wu, wd, pp, ps, gp, wpg, wple, *extra)


def _router_kernel(h_ref, g_ref, whi_ref, wlo_ref, meta_ref, cnt_ref, run_ref, *, n_exp):
    i = pl.program_id(0)

    @pl.when(i == 0)
    def _():
        run_ref[...] = jnp.zeros_like(run_ref)

    tm = h_ref.shape[0]
    xn = _rms(h_ref[...], g_ref[...])
    hi = xn.astype(BF16)
    lo = (xn - hi.astype(F32)).astype(BF16)
    logits = _dot(hi, whi_ref[...]) + (_dot(hi, wlo_ref[...]) + _dot(lo, whi_ref[...]))
    lane = lax.broadcasted_iota(jnp.int32, (tm, LANES), 1).astype(F32)
    logits = jnp.where(lane < n_exp, logits, -jnp.inf)
    m1 = jnp.max(logits, axis=-1, keepdims=True)
    i1 = jnp.min(jnp.where(logits == m1, lane, float(LANES)), axis=-1, keepdims=True)
    rest = jnp.where(lane == i1, -jnp.inf, logits)
    m2 = jnp.max(rest, axis=-1, keepdims=True)
    i2 = jnp.min(jnp.where(rest == m2, lane, float(LANES)), axis=-1, keepdims=True)
    e2 = jnp.exp(m2 - m1)
    w1 = 1.0 / (1.0 + e2)
    w2 = e2 / (1.0 + e2)
    oh1 = (lane == i1).astype(F32)
    oh2 = (lane == i2).astype(F32)
    oh = oh1 + oh2
    rows = lax.broadcasted_iota(jnp.int32, (tm, tm), 0)
    cols = lax.broadcasted_iota(jnp.int32, (tm, tm), 1)
    tri = jnp.where(cols < rows, 1.0, 0.0).astype(BF16)
    pre = _dot(tri, oh.astype(BF16)) + run_ref[...]
    r1 = jnp.sum(pre * oh1, axis=-1, keepdims=True)
    r2 = jnp.sum(pre * oh2, axis=-1, keepdims=True)
    run = run_ref[...] + jnp.sum(oh, axis=0, keepdims=True)
    run_ref[...] = run
    cnt_ref[...] = jnp.broadcast_to(run, cnt_ref.shape)
    meta = jnp.where(lane == 0, i1, 0.0)
    meta = jnp.where(lane == 1, i2, meta)
    meta = jnp.where(lane == 2, r1, meta)
    meta = jnp.where(lane == 3, r2, meta)
    meta = jnp.where(lane == 4, w1, meta)
    meta = jnp.where(lane == 5, w2, meta)
    meta_ref[...] = meta


def _router(h, g, whi, wlo, n_exp, tm):
    t, d = h.shape
    return pl.pallas_call(
        functools.partial(_router_kernel, n_exp=n_exp),
        out_shape=(jax.ShapeDtypeStruct((t, LANES), F32), jax.ShapeDtypeStruct((8, LANES), F32)),
        grid=(t // tm,),
        in_specs=[pl.BlockSpec((tm, d), lambda i: (i, 0)), _const_spec((1, d)),
                  _const_spec((d, LANES)), _const_spec((d, LANES))],
        out_specs=(pl.BlockSpec((tm, LANES), lambda i: (i, 0)), pl.BlockSpec((8, LANES), lambda i: (0, 0))),
        scratch_shapes=[pltpu.VMEM((1, LANES), F32)],
        compiler_params=_cparams(("arbitrary",)),
        name="router",
    )(h, g, whi, wlo)


def _inverse_kernel(pos_ref, lo_ref, hi_ref, inv_ref, *, n_rng):
    for e in range(n_rng):
        def fill(r, carry):
            inv_ref[r] = -1
            return carry

        lax.fori_loop(lo_ref[e], hi_ref[e], fill, 0)

    def put(idx, carry):
        inv_ref[pos_ref[idx]] = idx
        return carry

    lax.fori_loop(0, pos_ref.shape[0], put, 0, unroll=ISSUE_UNROLL)


def _moe_inverse(pos, lo, hi, rows):
    return pl.pallas_call(
        functools.partial(_inverse_kernel, n_rng=lo.shape[0]),
        out_shape=jax.ShapeDtypeStruct((rows,), jnp.int32),
        grid_spec=pltpu.PrefetchScalarGridSpec(
            num_scalar_prefetch=3, grid=(1,), in_specs=[],
            out_specs=pl.BlockSpec(memory_space=pltpu.SMEM)),
        compiler_params=_cparams(("arbitrary",)),
        name="moe_inverse",
    )(pos, lo, hi)


def _moe_perm_kernel(te_ref, inv_ref, h_ref, g_ref, wg_ref, wu_ref, wd_ref, o_ref,
                     xg_ref, yb_ref, act_ref, gsem, ssem, zsem, *, fc, t_tok, n_steps):
    i = pl.program_id(0)
    last = n_steps - 1
    trash = TOP_K * t_tok

    def gather_row(tile, slot, r):
        idx = inv_ref[tile * MOE_TILE + r]
        tok = lax.shift_right_logical(jnp.maximum(idx, 0), 1)
        pltpu.make_async_copy(h_ref.at[pl.ds(tok, 1)], xg_ref.at[slot, pl.ds(r, 1)], gsem.at[slot]).start()

    def scatter_row(tile, slot, r):
        idx = inv_ref[tile * MOE_TILE + r]
        dest = jnp.where(idx >= 0, (idx & 1) * t_tok + lax.shift_right_logical(idx, 1), trash + r)
        pltpu.make_async_copy(yb_ref.at[slot, pl.ds(r, 1)], o_ref.at[pl.ds(dest, 1)], ssem.at[slot]).start()

    def rolled(row_fn, tile, slot):
        def body(r, carry):
            row_fn(tile, slot, r)
            return carry

        lax.fori_loop(0, MOE_TILE, body, 0, unroll=ISSUE_UNROLL)

    def wait_gather(slot):
        pltpu.make_async_copy(h_ref.at[pl.ds(0, MOE_TILE)], xg_ref.at[slot], gsem.at[slot]).wait()

    def wait_scatter(slot):
        pltpu.make_async_copy(yb_ref.at[slot], o_ref.at[pl.ds(0, MOE_TILE)], ssem.at[slot]).wait()

    def compute(slot, issue_gathers, issue_scatters):
        f = wg_ref.shape[1]
        n_chunk = f // fc
        share = MOE_TILE // n_chunk
        x = _rms(xg_ref[slot], g_ref[...]).astype(BF16)
        issue_gathers()
        for c in range(n_chunk):
            cols = slice(c * fc, (c + 1) * fc)
            gate = _dot(x, wg_ref[:, cols])
            up = _dot(x, wu_ref[:, cols])
            if c == 0:
                row0 = pl.multiple_of(jnp.minimum(te_ref[0], 0) * 8, 8)
                xg_ref[slot, pl.ds(row0, 8), 0:LANES] = up[0:8, 0:LANES]
            act_ref[:, cols] = (_silu(gate) * up).astype(BF16)
            issue_scatters(c * share, MOE_TILE if c == n_chunk - 1 else (c + 1) * share)
        yb_ref[slot] = _dot(act_ref[...], wd_ref[...])

    @pl.when(i == 0)
    def _():
        yb_ref[1] = jnp.zeros(yb_ref.shape[1:], F32)
        cp = pltpu.make_async_copy(yb_ref.at[1], o_ref.at[pl.ds(trash, MOE_TILE)], zsem)
        cp.start()
        cp.wait()
        rolled(gather_row, 0, 0)
        wait_gather(0)
        def issue_gathers():
            for r in range(MOE_TILE):
                gather_row(1, 1, r)

        compute(0, issue_gathers, lambda r0, r1: None)

    for par in range(2):
        @pl.when((i > 0) & (i % 2 == par))
        def _():
            wait_gather(par)

            @pl.when(i >= 2)
            def _():
                wait_scatter(par)

            nxt = jnp.minimum(i + 1, last)

            def issue_gathers():
                for r in range(MOE_TILE):
                    gather_row(nxt, 1 - par, r)

            def issue_scatters(r0, r1):
                for r in range(r0, r1):
                    scatter_row(i - 1, 1 - par, r)

            compute(par, issue_gathers, issue_scatters)

    @pl.when(i == last)
    def _():
        wait_scatter(1 - last % 2)
        rolled(scatter_row, last, last % 2)
        wait_scatter(last % 2)
        wait_gather(1 - last % 2)


def _moe_perm_ffn(tile_e, inv, h, g, weg, weu, wed, layer):
    t, d = h.shape
    f = weg.shape[-1]
    n_steps = tile_e.shape[0]
    wmap = lambda i, te, inv_: (layer, te[i], 0, 0)
    buf = lambda: pltpu.VMEM((2, MOE_TILE, d), F32)
    return pl.pallas_call(
        functools.partial(_moe_perm_kernel, fc=_ff_chunk(f), t_tok=t, n_steps=n_steps),
        out_shape=jax.ShapeDtypeStruct((TOP_K * t + MOE_TILE, d), F32),
        grid_spec=pltpu.PrefetchScalarGridSpec(
            num_scalar_prefetch=2,
            grid=(n_steps,),
            in_specs=[pl.BlockSpec(memory_space=pl.ANY),
                      pl.BlockSpec((1, d), lambda i, te, inv_: (0, 0)),
                      pl.BlockSpec((None, None, d, f), wmap),
                      pl.BlockSpec((None, None, d, f), wmap),
                      pl.BlockSpec((None, None, f, d), wmap)],
            out_specs=pl.BlockSpec(memory_space=pl.ANY),
            scratch_shapes=[buf(), buf(), pltpu.VMEM((MOE_TILE, f), BF16),
                            pltpu.SemaphoreType.DMA((2,)), pltpu.SemaphoreType.DMA((2,)),
                            pltpu.SemaphoreType.DMA(())]),
        compiler_params=_cparams(("arbitrary",)),
        name="moe_perm_ffn",
    )(tile_e, inv, h, g, weg, weu, wed)


def _merge_kernel(h_ref, meta_ref, y0_ref, y1_ref, pp_ref, ps_ref, gp_ref, wpg_ref, wple_ref, gf_ref,
                  *outs, n_p, final):
    i = pl.program_id(0)
    meta = meta_ref[...]
    h = h_ref[...] + (meta[:, 4:5] * y0_ref[...] + meta[:, 5:6] * y1_ref[...])
    p = jnp.where(i < n_p, pp_ref[...], ps_ref[...])
    h = _ple(h, p, gp_ref[...], wpg_ref[...], wple_ref[...])
    if final:
        h = _rms(h, gf_ref[...])
        op_ref, os_ref = outs

        @pl.when(i < n_p)
        def _():
            op_ref[...] = h

        @pl.when(i >= n_p)
        def _():
            os_ref[...] = h
    else:
        outs[0][...] = h


def _merge_ple(h, meta, ys, pp, ps, layer, gp, wpg, wple, gf, tm, n_p, final):
    t, d = h.shape
    ple = pp.shape[-1]
    row = lambda off: pl.BlockSpec((tm, d), lambda i: (off + i, 0))
    if final:
        out_shape = (jax.ShapeDtypeStruct((n_p * tm, d), F32), jax.ShapeDtypeStruct((t - n_p * tm, d), F32))
        out_specs = (pl.BlockSpec((tm, d), lambda i: (jnp.minimum(i, n_p - 1), 0)),
                     pl.BlockSpec((tm, d), lambda i: (jnp.maximum(i - n_p, 0), 0)))
    else:
        out_shape = jax.ShapeDtypeStruct((t, d), F32)
        out_specs = row(0)
    return pl.pallas_call(
        functools.partial(_merge_kernel, n_p=n_p, final=final),
        out_shape=out_shape,
        grid=(t // tm,),
        in_specs=[row(0), pl.BlockSpec((tm, LANES), lambda i: (i, 0)), row(0), row(t // tm)]
                 + _p_specs(layer, tm, n_p, ple)
                 + [_const_spec((1, d)), _const_spec((d, d)), _const_spec((ple, d)), _const_spec((1, d))],
        out_specs=out_specs,
        compiler_params=_cparams(("arbitrary",)),
        name="moe_merge",
    )(h, meta, ys, ys, pp, ps, gp, wpg, wple, gf)


def _latent_kernel(h_ref, g_ref, w_ref, gc_ref, cos_ref, sin_ref, wk_ref, ek_ref, wv_ref, one_ref,
                   c_ref, kpe_ref, k_ref, v_ref):
    r = c_ref.shape[1]
    xn = _rms(h_ref[...], g_ref[...]).astype(BF16)
    z = _dot(xn, w_ref[...])
    c = _rms(z[:, :r], gc_ref[...])
    kpe = z[:, r:r + LANES] * cos_ref[...] + z[:, r + LANES:r + 2 * LANES] * sin_ref[...]
    c_ref[...] = c
    kpe_ref[...] = kpe
    cb = c.astype(BF16)
    k_ref[...] = (_dot(cb, wk_ref[...]) + _dot(kpe.astype(BF16), ek_ref[...])).astype(BF16)
    v_ref[...] = (_dot(cb, wv_ref[...]) + one_ref[...]).astype(BF16)


def _tbl_spec(tm, n_p, n_sblk):
    return pl.BlockSpec((tm, LANES), lambda i: (jnp.where(i < n_p, i % n_sblk, n_sblk + i - n_p), 0))


def _latent(h, g, w, gc, cos_t, sin_t, wk, ek, wv, ones, tm, n_p, n_sblk):
    t, d = h.shape
    r = gc.shape[1]
    hk = wk.shape[1]
    row = lambda w_: pl.BlockSpec((tm, w_), lambda i: (i, 0))
    return pl.pallas_call(
        _latent_kernel,
        out_shape=(jax.ShapeDtypeStruct((t, r), F32), jax.ShapeDtypeStruct((t, LANES), F32),
                   jax.ShapeDtypeStruct((t, hk), BF16), jax.ShapeDtypeStruct((t, hk), BF16)),
        grid=(t // tm,),
        in_specs=[row(d), _const_spec((1, d)), _const_spec(w.shape), _const_spec((1, r)),
                  _tbl_spec(tm, n_p, n_sblk), _tbl_spec(tm, n_p, n_sblk),
                  _const_spec(wk.shape), _const_spec(ek.shape), _const_spec(wv.shape), _const_spec(ones.shape)],
        out_specs=(row(r), row(LANES), row(hk), row(hk)),
        compiler_params=_cparams(("parallel",)),
        name="mla_latent",
    )(h, g, w, gc, cos_t, sin_t, wk, ek, wv, ones)


def _query_kernel(h_ref, g_ref, wdq_ref, gq_ref, wq_ref, wqr_ref, cos_ref, sin_ref, q_ref, *, scale):
    xn = _rms(h_ref[...], g_ref[...]).astype(BF16)
    cq = _rms(_dot(xn, wdq_ref[...]), gq_ref[...]).astype(BF16)
    q = _dot(cq, wq_ref[...])
    qr = _dot(cq, wqr_ref[...])
    cos = cos_ref[...] * scale
    sin = sin_ref[...] * scale
    for hd in range(q.shape[1] // LANES):
        lanes = slice(hd * LANES, (hd + 1) * LANES)
        q_ref[:, lanes] = (q[:, lanes] * cos + qr[:, lanes] * sin).astype(BF16)


def _queries(h, g, wdq, gq, wq, wqr, cos_t, sin_t, scale, tm, n_p, n_sblk):
    t, d = h.shape
    hk = wq.shape[1]
    return pl.pallas_call(
        functools.partial(_query_kernel, scale=scale),
        out_shape=jax.ShapeDtypeStruct((t, hk), BF16),
        grid=(t // tm,),
        in_specs=[pl.BlockSpec((tm, d), lambda i: (i, 0)), _const_spec((1, d)), _const_spec(wdq.shape),
                  _const_spec(gq.shape), _const_spec(wq.shape), _const_spec(wqr.shape),
                  _tbl_spec(tm, n_p, n_sblk), _tbl_spec(tm, n_p, n_sblk)],
        out_specs=pl.BlockSpec((tm, hk), lambda i: (i, 0)),
        compiler_params=_cparams(("parallel",)),
        name="mla_queries",
    )(h, g, wdq, gq, wq, wqr, cos_t, sin_t)


def _attn_kernel(q_ref, k_ref, v_ref, o_ref, s00, s01, s10, s11, acc0, acc1, m0, m1, *, tq, tk):
    qi = pl.program_id(2)
    half = LANES // 2
    s_refs = ((s00, s01), (s10, s11))
    acc_refs = (acc0, acc1)
    m_refs = (m0, m1)
    lanes = (slice(0, LANES), slice(LANES, 2 * LANES))
    shift = CHUNK.bit_length() - 1

    def produce(hd, slot, j):
        start = pl.multiple_of(j * tk, tk)
        s_refs[hd][slot][...] = _dot_nt(q_ref[:, lanes[hd]], k_ref[pl.ds(start, tk), lanes[hd]])

    def consume(hd, slot, j, masked):
        start = pl.multiple_of(j * tk, tk)
        s = s_refs[hd][slot][...]
        if masked:
            qpos = qi * tq + lax.broadcasted_iota(jnp.int32, (tq, tk), 0)
            kpos = j * tk + lax.broadcasted_iota(jnp.int32, (tq, tk), 1)
            s = jnp.where(lax.shift_right_logical(kpos, shift) <= lax.shift_right_logical(qpos, shift), s, NEG)
        m_old = m_refs[hd][...]
        m_new = jnp.maximum(m_old, jnp.broadcast_to(jnp.max(s, axis=-1, keepdims=True), m_old.shape))
        p = jnp.exp2(s - jnp.concatenate([m_new] * (tk // LANES), axis=1)).astype(BF16)
        acc_refs[hd][...] = (jnp.exp2(m_old - m_new) * acc_refs[hd][...]
                             + _dot(p, v_ref[pl.ds(start, tk), lanes[hd]]))
        m_refs[hd][...] = m_new

    for hd in range(2):
        m_refs[hd][...] = jnp.full(m_refs[hd].shape, NEG, F32)
        acc_refs[hd][...] = jnp.zeros(acc_refs[hd].shape, F32)
        produce(hd, 0, 0)

    assert tq == tk

    def pair(jj, carry):
        a = 2 * jj
        for hd in range(2):
            produce(hd, 1, a + 1)
        for hd in range(2):
            consume(hd, 0, a, False)
        for hd in range(2):
            produce(hd, 0, a + 2)
        for hd in range(2):
            consume(hd, 1, a + 1, False)
        return carry

    lax.fori_loop(0, qi // 2, pair, 0)
    a = 2 * (qi // 2)

    @pl.when(qi % 2 == 0)
    def _():
        for hd in range(2):
            consume(hd, 0, a, True)

    @pl.when(qi % 2 == 1)
    def _():
        for hd in range(2):
            produce(hd, 1, a + 1)
        for hd in range(2):
            consume(hd, 0, a, False)
        for hd in range(2):
            consume(hd, 1, a + 1, True)

    acc_e, acc_o = acc0[...], acc1[...]
    o_e = acc_e / acc_e[:, half:half + 1]
    o_o = acc_o / acc_o[:, 0:1]
    lane = lax.broadcasted_iota(jnp.int32, (tq, LANES), 1)
    o_ref[...] = jnp.where(lane < half, o_e, o_o).astype(BF16)


def _prompt_attention(q, k, v, n_b, seq, n_heads, tq, tk):
    nq = seq // tq
    hv = n_heads * (LANES // 2)
    return pl.pallas_call(
        functools.partial(_attn_kernel, tq=tq, tk=tk),
        out_shape=jax.ShapeDtypeStruct((n_b * seq, hv), BF16),
        grid=(n_b, n_heads // 2, nq),
        in_specs=[pl.BlockSpec((tq, 2 * LANES), lambda b, hp, i: (b * nq + i, hp)),
                  pl.BlockSpec((seq, 2 * LANES), lambda b, hp, i: (b, hp)),
                  pl.BlockSpec((seq, 2 * LANES), lambda b, hp, i: (b, hp))],
        out_specs=pl.BlockSpec((tq, LANES), lambda b, hp, i: (b * nq + i, hp)),
        scratch_shapes=[pltpu.VMEM((tq, tk), F32)] * 4 + [pltpu.VMEM((tq, LANES), F32)] * 2
                       + [pltpu.VMEM((tq, LANES), F32)] * 2,
        compiler_params=_cparams(("parallel", "parallel", "arbitrary")),
        name="prompt_attention",
    )(q, k, v)


def _qfeat_kernel(q_ref, w_ref, o_ref):
    o_ref[...] = _dot(q_ref[...], w_ref[...]).astype(BF16)


def _sample_qfeat(q, wq2f, row0_blk, ts):
    n_heads, _, fw = wq2f.shape
    return pl.pallas_call(
        _qfeat_kernel,
        out_shape=jax.ShapeDtypeStruct((n_heads, ts, fw), BF16),
        grid=(n_heads,),
        in_specs=[pl.BlockSpec((ts, LANES), lambda hd: (row0_blk, hd)),
                  pl.BlockSpec((None, LANES, fw), lambda hd: (hd, 0, 0))],
        out_specs=pl.BlockSpec((None, ts, fw), lambda hd: (hd, 0, 0)),
        compiler_params=_cparams(("parallel",)),
        name="sample_qfeat",
    )(q, wq2f)


def _sample_attn_kernel(qf_ref, cc_ref, ckpe_ref, cn_ref, kn_ref, e_ref, o_ref):
    n_heads, sq, fw = qf_ref.shape
    r = cc_ref.shape[1]
    qf = qf_ref[...].reshape(n_heads * sq, fw)
    qc, qp = qf[:, :r], qf[:, r:]
    kc = cc_ref[...].astype(BF16)
    kp = _dot(ckpe_ref[...].astype(BF16), e_ref[...]).astype(BF16)
    kcn = cn_ref[...].astype(BF16)
    kpn = kn_ref[...].astype(BF16)
    s1 = _dot_nt(qc, kc) + _dot_nt(qp, kp)
    s2 = _dot_nt(qc, kcn) + _dot_nt(qp, kpn)
    m = jnp.maximum(jnp.max(s1, axis=-1, keepdims=True), jnp.max(s2, axis=-1, keepdims=True))
    p1 = jnp.exp2(s1 - m)
    p2 = jnp.exp2(s2 - m)
    l = jnp.sum(p1, axis=-1, keepdims=True) + jnp.sum(p2, axis=-1, keepdims=True)
    o = (_dot(p1.astype(BF16), kc) + _dot(p2.astype(BF16), kcn)) / l
    o_ref[...] = o.astype(BF16).reshape(n_heads, sq, r)


def _sample_attention(qf, cache_c, cache_kpe, c_all, kpe_all, e_place, tp, sq):
    n_heads, ts, fw = qf.shape
    n_bs, past, r = cache_c.shape
    rope = cache_kpe.shape[-1]
    return pl.pallas_call(
        _sample_attn_kernel,
        out_shape=jax.ShapeDtypeStruct((n_heads, ts, r), BF16),
        grid=(n_bs,),
        in_specs=[pl.BlockSpec((n_heads, sq, fw), lambda b: (0, b, 0)),
                  pl.BlockSpec((None, past, r), lambda b: (b, 0, 0)),
                  pl.BlockSpec((None, past, rope), lambda b: (b, 0, 0)),
                  pl.BlockSpec((sq, r), lambda b: (tp // sq + b, 0)),
                  pl.BlockSpec((sq, LANES), lambda b: (tp // sq + b, 0)),
                  _const_spec(e_place.shape)],
        out_specs=pl.BlockSpec((n_heads, sq, r), lambda b: (0, b, 0)),
        compiler_params=_cparams(("parallel",)),
        name="sample_attention",
    )(qf, cache_c, cache_kpe, c_all, kpe_all, e_place)


def _sample_uv_kernel(ol_ref, wv_ref, o_ref):
    o_ref[...] = (_dot(ol_ref[0], wv_ref[0]) + _dot(ol_ref[1], wv_ref[1])).astype(BF16)


def _sample_uv(o_lat, wv_pair):
    n_heads, ts, r = o_lat.shape
    return pl.pallas_call(
        _sample_uv_kernel,
        out_shape=jax.ShapeDtypeStruct((ts, n_heads * (LANES // 2)), BF16),
        grid=(n_heads // 2,),
        in_specs=[pl.BlockSpec((2, ts, r), lambda hp: (hp, 0, 0)),
                  pl.BlockSpec((2, r, LANES), lambda hp: (hp, 0, 0))],
        out_specs=pl.BlockSpec((ts, LANES), lambda hp: (0, hp)),
        compiler_params=_cparams(("parallel",)),
        name="sample_uv",
    )(o_lat, wv_pair)


def _oproj_kernel(op_ref, os_ref, w_ref, h_ref, out_ref, *, n_p):
    o = jnp.where(pl.program_id(0) < n_p, op_ref[...], os_ref[...])
    out_ref[...] = h_ref[...] + _dot(o, w_ref[...])


def _oproj(o_p, o_s, w, h, tm, n_p):
    t, d = h.shape
    hv = o_p.shape[1]
    return pl.pallas_call(
        functools.partial(_oproj_kernel, n_p=n_p),
        out_shape=jax.ShapeDtypeStruct((t, d), F32),
        grid=(t // tm,),
        in_specs=[pl.BlockSpec((tm, hv), lambda i: (jnp.minimum(i, n_p - 1), 0)),
                  pl.BlockSpec((tm, hv), lambda i: (jnp.maximum(i - n_p, 0), 0)),
                  _const_spec(w.shape),
                  pl.BlockSpec((tm, d), lambda i: (i, 0))],
        out_specs=pl.BlockSpec((tm, d), lambda i: (i, 0)),
        compiler_params=_cparams(("parallel",)),
        name="attn_oproj",
    )(o_p, o_s, w, h)


def _rot_cols(w):
    half = w.shape[-1] // 2
    return jnp.concatenate([-w[..., half:], w[..., :half]], axis=-1)


def _moe_schedule(meta, cnt, n_exp, nt_max):
    counts = cnt[0, :n_exp].astype(jnp.int32)
    ntile = (counts + MOE_TILE - 1) // MOE_TILE
    tile_end = jnp.cumsum(ntile)
    seg_start = (tile_end - ntile) * MOE_TILE
    n_tiles = tile_end[-1]
    e = meta[:, 0:TOP_K].astype(jnp.int32)
    rank = meta[:, TOP_K:2 * TOP_K].astype(jnp.int32)
    pos = (seg_start[e] + rank).reshape(-1)
    tid = jnp.minimum(jnp.arange(nt_max, dtype=jnp.int32), n_tiles - 1)
    tile_e = jnp.sum((tid[:, None] >= tile_end[None, :]).astype(jnp.int32), axis=1)
    lo = jnp.concatenate([seg_start + counts, (n_tiles * MOE_TILE).reshape(1)]).astype(jnp.int32)
    hi = jnp.concatenate([tile_end * MOE_TILE, jnp.full((1,), nt_max * MOE_TILE)]).astype(jnp.int32)
    return pos.astype(jnp.int32), tile_e.astype(jnp.int32), lo, hi


def kernel(x_prompt, x_sample, state_conv, cache_ckv, cache_kpe, p_prompt, p_sample, g_mix, g_ffn, w_pw1, b_pw1, w_dw, b_dw, ln_g, ln_b, w_pw2, b_pw2, g_kv, w_dkv, g_ckv, w_uk, w_uv, w_dq, g_q, w_uq, w_o, w_gate, w_up, w_down, w_router, we_gate, we_up, we_down, w_ple, w_ple_gate, g_ple, g_final):
    n_b, seq, d = x_prompt.shape
    n_bs, seq_s, _ = x_sample.shape
    depth = g_mix.shape[0]
    n_a = w_pw1.shape[0]
    taps = w_dw.shape[1]
    past = cache_ckv.shape[1]
    r_kv = cache_ckv.shape[2]
    rope = cache_kpe.shape[2]
    n_heads, nope = w_uk.shape[1], w_uk.shape[2]
    v_dim = w_uv.shape[2]
    n_exp = w_router.shape[2]
    ple = p_prompt.shape[-1]
    tp, ts = n_b * seq, n_bs * seq_s
    t = tp + ts
    tm = ts
    assert tp % tm == 0 and seq % tm == 0 and tm % MOE_TILE == 0
    assert nope == LANES // 2 and v_dim == LANES // 2 and nope + rope <= LANES and taps - 1 <= HALO
    n_p = tp // tm
    n_sblk = seq // tm
    scale = float(nope + rope) ** -0.5 * math.log2(math.e)
    row = lambda a: a.reshape(1, -1)
    bf = lambda a: a.astype(BF16)

    h = jnp.concatenate([x_prompt.reshape(tp, d), x_sample.reshape(ts, d)], axis=0)
    pp = p_prompt.reshape(depth, tp, ple)
    ps = p_sample.reshape(depth, ts, ple)

    posn = jnp.concatenate([jnp.arange(seq), past + (jnp.arange(ts) % seq_s)]).astype(F32)
    inv = ROPE_THETA ** (-jnp.arange(0, rope, 2, dtype=F32) / rope)
    ang = posn[:, None] * inv[None, :]
    ang = jnp.concatenate([ang, ang], axis=-1)
    pad_r = LANES - nope - rope
    cos_t = jnp.concatenate([jnp.ones((posn.shape[0], nope), F32), jnp.cos(ang),
                             jnp.zeros((posn.shape[0], pad_r), F32)], axis=1)
    sin_t = jnp.concatenate([jnp.zeros((posn.shape[0], nope), F32), jnp.sin(ang),
                             jnp.zeros((posn.shape[0], pad_r), F32)], axis=1)

    conv_p, conv_s = [], []
    c_all = kpe_all = k_cat = v_cat = None
    nt_max = (TOP_K * t) // MOE_TILE + n_exp
    rows_pad = nt_max * MOE_TILE

    for i in range(depth):
        if i < n_a:
            a = i
            u = _pw1_glu(h, row(g_mix[i]), bf(w_pw1[a]), row(b_pw1[a]), tm)
            st_pad = jnp.pad(state_conv[a], ((0, 0), (HALO - (taps - 1), 0), (0, 0))).reshape(n_bs * HALO, d)
            h = _conv_block(u, h, st_pad, w_dw[a], row(b_dw[a]), row(ln_g[a]), row(ln_b[a]),
                            bf(w_pw2[a]), row(b_pw2[a]), n_b=n_b, seq=seq, n_bs=n_bs, seq_s=seq_s, tt=tm)
            conv_p.append(jnp.stack([u[(bi + 1) * seq - (taps - 1):(bi + 1) * seq] for bi in range(n_b)]))
            us = u[tp:].reshape(n_bs, seq_s, d)
            conv_s.append(jnp.concatenate([state_conv[a], us], axis=1)[:, -(taps - 1):])
        else:
            b = i - n_a
            wq = w_uq[b]
            zq = jnp.zeros(wq.shape[:2] + (pad_r,), F32)
            wq_cat = jnp.concatenate([wq, zq], axis=-1).reshape(wq.shape[0], n_heads * LANES)
            wq_rot = jnp.concatenate([jnp.zeros(wq.shape[:2] + (nope,), F32), _rot_cols(wq[..., nope:]), zq],
                                     axis=-1).reshape(wq.shape[0], n_heads * LANES)
            q = _queries(h, row(g_mix[i]), bf(w_dq[b]), row(g_q[b]), bf(wq_cat), bf(wq_rot),
                         cos_t, sin_t, scale, tm, n_p, n_sblk)
            o_p = _prompt_attention(q, k_cat, v_cat, n_b, seq, n_heads, tq=tm, tk=tm)
            wk_t = jnp.transpose(w_uk, (1, 2, 0))
            sel = jnp.zeros((LANES, LANES), F32).at[jnp.arange(nope, nope + rope), jnp.arange(nope, nope + rope)].set(1.0)
            wq2f = jnp.concatenate([
                jnp.concatenate([wk_t, jnp.zeros((n_heads, LANES - nope, r_kv), F32)], axis=1),
                jnp.broadcast_to(sel, (n_heads, LANES, LANES))], axis=2)
            qf = _sample_qfeat(q, bf(wq2f), n_p, ts)
            e_place = jnp.zeros((rope, LANES), F32).at[jnp.arange(rope), nope + jnp.arange(rope)].set(1.0)
            o_lat = _sample_attention(qf, cache_ckv, cache_kpe, c_all, kpe_all, bf(e_place), tp, seq_s)
            wv_h = jnp.transpose(w_uv, (1, 0, 2))
            zv = jnp.zeros_like(wv_h)
            wv_pair = jnp.where((jnp.arange(n_heads) % 2 == 0)[:, None, None],
                                jnp.concatenate([wv_h, zv], axis=2), jnp.concatenate([zv, wv_h], axis=2))
            o_s = _sample_uv(o_lat, bf(wv_pair))
            attn = (o_p, o_s, bf(w_o[b].reshape(n_heads * v_dim, d)))
            if i % 2 == 1:
                h = _oproj(*attn, h, tm, n_p)
                attn = None

        j = i // 2
        last = i == depth - 1
        if i % 2 == 0:
            h = _ffn_ple(h, row(g_ffn[i]), bf(w_gate[j]), bf(w_up[j]), bf(w_down[j]), pp, ps, i,
                         row(g_ple[i]), bf(w_ple_gate[i]), bf(w_ple[i]), tm, n_p,
                         attn=attn if i >= n_a else None)
        else:
            wr = jnp.pad(w_router[j], ((0, 0), (0, LANES - n_exp)))
            wr_hi = bf(wr)
            wr_lo = bf(wr - wr_hi.astype(F32))
            meta, cnt = _router(h, row(g_ffn[i]), wr_hi, wr_lo, n_exp, tm)
            pos, tile_e, pad_lo, pad_hi = _moe_schedule(meta, cnt, n_exp, nt_max)
            inv = _moe_inverse(pos, pad_lo, pad_hi, rows_pad)
            ys = _moe_perm_ffn(tile_e, inv, h, row(g_ffn[i]), bf(we_gate), bf(we_up), bf(we_down), j)
            h = _merge_ple(h, meta, ys, pp, ps, i, row(g_ple[i]), bf(w_ple_gate[i]), bf(w_ple[i]),
                           row(g_final), tm, n_p, final=last)
        if last and i % 2 == 0:
            raise NotImplementedError("final norm is fused into the MoE combine of the last layer")

        if i == n_a - 1:
            w_ext = jnp.concatenate([
                w_dkv[:, :r_kv],
                jnp.zeros((d, nope), F32), w_dkv[:, r_kv:], jnp.zeros((d, pad_r), F32),
                jnp.zeros((d, nope), F32), _rot_cols(w_dkv[:, r_kv:]), jnp.zeros((d, pad_r), F32)], axis=1)
            wk_cat = jnp.concatenate([w_uk, jnp.zeros((r_kv, n_heads, LANES - nope), F32)], axis=2)
            wk_cat = wk_cat.reshape(r_kv, n_heads * LANES)
            lane = jnp.arange(LANES)
            e_k = ((lane[:, None] == (jnp.arange(n_heads * LANES) % LANES)[None, :])
                   & (lane[:, None] >= nope) & (lane[:, None] < nope + rope)).astype(F32)
            zv = jnp.zeros_like(w_uv)
            wv_cat = jnp.where((jnp.arange(n_heads) % 2 == 0)[None, :, None],
                               jnp.concatenate([w_uv, zv], axis=2), jnp.concatenate([zv, w_uv], axis=2))
            wv_cat = wv_cat.reshape(r_kv, n_heads * LANES)
            col = jnp.arange(n_heads * LANES)
            ones_lane = jnp.where((col // LANES) % 2 == 0, v_dim, 0)
            v_ones = (col % LANES == ones_lane).astype(F32).reshape(1, -1)
            c_all, kpe_all, k_cat, v_cat = _latent(h, row(g_kv), bf(w_ext), row(g_ckv), cos_t, sin_t,
                                                   bf(wk_cat), bf(e_k), bf(wv_cat), v_ones, tm, n_p, n_sblk)

    kpe_out = kpe_all[:, nope:nope + rope]
    y_p, y_s = h
    return (y_p.reshape(n_b, seq, d), y_s.reshape(n_bs, seq_s, d),
            jnp.stack(conv_p), jnp.stack(conv_s),
            c_all[:tp].reshape(n_b, seq, r_kv), kpe_out[:tp].reshape(n_b, seq, rope),
            c_all[tp:].reshape(n_bs, seq_s, r_kv), kpe_out[tp:].reshape(n_bs, seq_s, rope))
```
